```python
import math
import jax, jax.numpy as jnp
from jax import lax
import numpy as np

D_MODEL = 1024
BATCH = 1
SEQ = 16384
DEPTH = 4
DEC_BATCH = 8
DEC_SEQ = 32
PAST_LEN = 4096

CHUNK = 64
N_HEADS = 8
HEAD_DIM = 64
D_ATTN = N_HEADS * 2 * HEAD_DIM
D_CONV = D_MODEL
CONV_WIDTH = 31
CONV_STATE = CONV_WIDTH - 1
D_FF = -(-8 * D_MODEL // (3 * 256)) * 256
N_BUCKETS = 32
MAX_DISTANCE = 128
Q_BLOCK = 128
RMS_EPS = 1e-6
LN_EPS = 1e-5
SUBLN_EPS = 1e-5
NEG_INF = -1e30

OFF_Q = 2 * D_CONV
OFF_K = OFF_Q + D_ATTN
OFF_V = OFF_K + D_ATTN
OFF_GATE = OFF_V + D_ATTN
D_IN = OFF_GATE + 2 * D_MODEL

kernel_name = "hybrid_conformer_conv_diff_attn_stream_step"


def rmsnorm(x, g, eps=RMS_EPS):
    xf = x.astype(jnp.float32)
    y = xf * lax.rsqrt(jnp.mean(xf * xf, axis=-1, keepdims=True) + eps)
    return (y * g.astype(jnp.float32)).astype(x.dtype)


def layernorm(x, g, b):
    xf = x.astype(jnp.float32)
    mu = jnp.mean(xf, axis=-1, keepdims=True)
    var = jnp.mean(jnp.square(xf - mu), axis=-1, keepdims=True)
    y = (xf - mu) * lax.rsqrt(var + LN_EPS)
    return (y * g.astype(jnp.float32) + b.astype(jnp.float32)).astype(x.dtype)


def t5_bucket(rel):
    half = N_BUCKETS // 2
    max_exact = half // 2
    ret = jnp.where(rel > 0, half, 0)
    n = jnp.abs(rel)
    nf = jnp.maximum(n, 1).astype(jnp.float32)
    large = max_exact + (jnp.log(nf / max_exact) / math.log(MAX_DISTANCE / max_exact)
                         * (half - max_exact)).astype(jnp.int32)
    large = jnp.minimum(large, half - 1)
    return ret + jnp.where(n < max_exact, n, large)


def diff_attn_block(q, k, v, q_pos, k_pos, lam, rel_bias):
    s = jnp.einsum('bqhmd,bkhmd->bhmqk', q, k,
                   preferred_element_type=jnp.float32) * (HEAD_DIM ** -0.5)
    bucket = t5_bucket(k_pos[None, :] - q_pos[:, None])
    bias = jnp.transpose(rel_bias.astype(jnp.float32)[bucket], (2, 0, 1))
    allowed = (k_pos[None, :] // CHUNK) <= (q_pos[:, None] // CHUNK)
    s = jnp.where(allowed, s + bias[None, :, None], NEG_INF)
    p = jax.nn.softmax(s, axis=-1)
    a = p[:, :, 0] - lam * p[:, :, 1]
    return jnp.einsum('bhqk,bkhe->bqhe', a.astype(v.dtype), v)


def diff_attention(q, k, v, q_pos, k_pos, lam, rel_bias):
    B, Tq = q.shape[0], q.shape[1]
    if Tq > Q_BLOCK and Tq % Q_BLOCK == 0:
        nb = Tq // Q_BLOCK
        qb = jnp.moveaxis(q.reshape(B, nb, Q_BLOCK, N_HEADS, 2, HEAD_DIM), 1, 0)
        pb = q_pos.reshape(nb, Q_BLOCK)
        ob = lax.map(lambda a: diff_attn_block(a[0], k, v, a[1], k_pos, lam, rel_bias), (qb, pb))
        return jnp.moveaxis(ob, 0, 1).reshape(B, Tq, N_HEADS, 2 * HEAD_DIM)
    return diff_attn_block(q, k, v, q_pos, k_pos, lam, rel_bias)


def layer(l, x, q_pos, k_past, v_past, conv_past, p):
    B, T = x.shape[0], x.shape[1]
    h = rmsnorm(x, p['norm_mix'][l])
    z = h @ p['w_in'][l]

    u = z[..., :OFF_Q]
    glu = u[..., :D_CONV] * jax.nn.sigmoid(u[..., D_CONV:])
    padded = jnp.concatenate([conv_past.astype(glu.dtype), glu], axis=1)
    dw = lax.conv_general_dilated(
        padded, p['conv_dw'][l][:, None, :].astype(padded.dtype), (1,), 'VALID',
        dimension_numbers=('NWC', 'WIO', 'NWC'), feature_group_count=D_CONV)
    dw = dw + p['conv_dw_b'][l]
    c = jax.nn.silu(layernorm(dw, p['conv_ln_g'][l], p['conv_ln_b'][l]))
    a_out = c @ p['w_conv_out'][l]
    conv_new = padded[:, -CONV_STATE:]

    q = z[..., OFF_Q:OFF_K].reshape(B, T, N_HEADS, 2, HEAD_DIM)
    k_new = z[..., OFF_K:OFF_V].reshape(B, T, N_HEADS, 2 * HEAD_DIM)
    v_new = z[..., OFF_V:OFF_GATE].reshape(B, T, N_HEADS, 2 * HEAD_DIM)
    if k_past is None:
        k_all, v_all, k_pos = k_new, v_new, q_pos
    else:
        k_all = jnp.concatenate([k_past.astype(k_new.dtype), k_new], axis=1)
        v_all = jnp.concatenate([v_past.astype(v_new.dtype), v_new], axis=1)
        k_pos = jnp.arange(k_all.shape[1], dtype=jnp.int32)
    Tk = k_all.shape[1]
    lambda_init = 0.8 - 0.6 * math.exp(-0.3 * l)
    lam = (jnp.exp(jnp.sum(p['lambda_q1'][l].astype(jnp.float32) * p['lambda_k1'][l].astype(jnp.float32)))
           - jnp.exp(jnp.sum(p['lambda_q2'][l].astype(jnp.float32) * p['lambda_k2'][l].astype(jnp.float32)))
           + lambda_init)
    o = diff_attention(q, k_all.reshape(B, Tk, N_HEADS, 2, HEAD_DIM), v_all,
                       q_pos, k_pos, lam, p['rel_bias'])
    o = rmsnorm(o, p['attn_subln_g'][l], SUBLN_EPS) * (1.0 - lambda_init)
    b_out = o.reshape(B, T, D_ATTN) @ p['w_attn_o'][l]

    gates = jax.nn.sigmoid(z[..., OFF_GATE:])
    merged = gates[..., :D_MODEL] * a_out + gates[..., D_MODEL:] * b_out
    x = x + merged @ p['w_out'][l]

    h2 = rmsnorm(x, p['norm_ffn'][l])
    gu = h2 @ p['w_ffn_in'][l]
    x = x + (jax.nn.silu(gu[..., :D_FF]) * gu[..., D_FF:]) @ p['w_ffn_out'][l]
    return x, k_new, v_new, conv_new


def setup_inputs(seed: int = 0) -> dict:
    key = jax.random.key(seed)
    ks = jax.random.split(key, 24)

    def nrm(k, shape, scale):
        return jax.random.normal(k, shape, jnp.float32) * scale

    return {
        "x_prompt": nrm(ks[0], (BATCH, SEQ, D_MODEL), 1.0),
        "x_sample": nrm(ks[1], (DEC_BATCH, DEC_SEQ, D_MODEL), 1.0),
        "cache_k": nrm(ks[2], (DEPTH, DEC_BATCH, PAST_LEN, N_HEADS, 2 * HEAD_DIM), 1.0),
        "cache_v": nrm(ks[3], (DEPTH, DEC_BATCH, PAST_LEN, N_HEADS, 2 * HEAD_DIM), 1.0),
        "state_conv": nrm(ks[4], (DEPTH, DEC_BATCH, CONV_STATE, D_CONV), 0.5),
        "norm_mix": 1.0 + nrm(ks[5], (DEPTH, D_MODEL), 0.02),
        "w_in": nrm(ks[6], (DEPTH, D_MODEL, D_IN), D_MODEL ** -0.5),
        "conv_dw": nrm(ks[7], (DEPTH, CONV_WIDTH, D_CONV), CONV_WIDTH ** -0.5),
        "conv_dw_b": nrm(ks[8], (DEPTH, D_CONV), 0.02),
        "conv_ln_g": 1.0 + nrm(ks[9], (DEPTH, D_CONV), 0.02),
        "conv_ln_b": nrm(ks[10], (DEPTH, D_CONV), 0.02),
        "w_conv_out": nrm(ks[11], (DEPTH, D_CONV, D_MODEL), D_CONV ** -0.5),
        "lambda_q1": nrm(ks[12], (DEPTH, HEAD_DIM), 0.1),
        "lambda_k1": nrm(ks[13], (DEPTH, HEAD_DIM), 0.1),
        "lambda_q2": nrm(ks[14], (DEPTH, HEAD_DIM), 0.1),
        "lambda_k2": nrm(ks[15], (DEPTH, HEAD_DIM), 0.1),
        "attn_subln_g": 1.0 + nrm(ks[16], (DEPTH, 2 * HEAD_DIM), 0.02),
        "w_attn_o": nrm(ks[17], (DEPTH, D_ATTN, D_MODEL), D_ATTN ** -0.5),
        "w_out": nrm(ks[18], (DEPTH, D_MODEL, D_MODEL), D_MODEL ** -0.5),
        "norm_ffn": 1.0 + nrm(ks[19], (DEPTH, D_MODEL), 0.02),
        "w_ffn_in": nrm(ks[20], (DEPTH, D_MODEL, 2 * D_FF), D_MODEL ** -0.5),
        "w_ffn_out": nrm(ks[21], (DEPTH, D_FF, D_MODEL), D_FF ** -0.5),
        "rel_bias": nrm(ks[22], (N_BUCKETS, N_HEADS), 0.5),
        "norm_final": 1.0 + nrm(ks[23], (D_MODEL,), 0.02),
    }


def reference(x_prompt, x_sample, cache_k, cache_v, state_conv,
              norm_mix, w_in, conv_dw, conv_dw_b, conv_ln_g, conv_ln_b, w_conv_out,
              lambda_q1, lambda_k1, lambda_q2, lambda_k2, attn_subln_g, w_attn_o,
              w_out, norm_ffn, w_ffn_in, w_ffn_out, rel_bias, norm_final):
    p = dict(norm_mix=norm_mix, w_in=w_in, conv_dw=conv_dw, conv_dw_b=conv_dw_b,
             conv_ln_g=conv_ln_g, conv_ln_b=conv_ln_b, w_conv_out=w_conv_out,
             lambda_q1=lambda_q1, lambda_k1=lambda_k1, lambda_q2=lambda_q2,
             lambda_k2=lambda_k2, attn_subln_g=attn_subln_g, w_attn_o=w_attn_o,
             w_out=w_out, norm_ffn=norm_ffn, w_ffn_in=w_ffn_in, w_ffn_out=w_ffn_out,
             rel_bias=rel_bias)

    T_p = x_prompt.shape[1]
    pos_p = jnp.arange(T_p, dtype=jnp.int32)
    conv0 = jnp.zeros((x_prompt.shape[0], CONV_STATE, D_CONV), x_prompt.dtype)
    xp = x_prompt
    kp, vp, cp = [], [], []
    for l in range(DEPTH):
        xp, k_n, v_n, c_n = layer(l, xp, pos_p, None, None, conv0, p)
        kp.append(k_n); vp.append(v_n); cp.append(c_n)
    y_prompt = rmsnorm(xp, norm_final)

    P = cache_k.shape[2]
    T_s = x_sample.shape[1]
    pos_s = P + jnp.arange(T_s, dtype=jnp.int32)
    xs = x_sample
    ks_, vs_, cs_ = [], [], []
    for l in range(DEPTH):
        xs, k_n, v_n, c_n = layer(l, xs, pos_s, cache_k[l], cache_v[l], state_conv[l], p)
        ks_.append(k_n); vs_.append(v_n); cs_.append(c_n)
    y_sample = rmsnorm(xs, norm_final)

    k_prompt = jnp.stack(kp)
    v_prompt = jnp.stack(vp)
    conv_prompt = jnp.stack(cp)
    k_sample = jnp.stack(ks_)
    v_sample = jnp.stack(vs_)
    conv_sample = jnp.stack(cs_)
    return (y_prompt, y_sample, k_prompt, v_prompt, conv_prompt, k_sample, v_sample, conv_sample)
```

```python
import functools
import math

import jax
import jax.numpy as jnp
from jax import lax
from jax.experimental import pallas as pl
from jax.experimental.pallas import tpu as pltpu

N_HEADS = 8
HEAD_DIM = 64
HEAD_COLS = 2 * HEAD_DIM
CONV_WIDTH = 31
CONV_STATE = CONV_WIDTH - 1
CHUNK = 64
N_BUCKETS = 32
MAX_DISTANCE = 128
RMS_EPS = 1e-6
LN_EPS = 1e-5
SUBLN_EPS = 1e-5
NEG_INF = -1e30

HALO_ROWS = 32
V7X_VMEM_LIMIT_BYTES = 56 * 1024 * 1024

_F32 = jnp.float32
_BF16 = jnp.bfloat16


def _pick_tile(n, candidates):
    for c in candidates:
        if n % c == 0:
            return c
    raise ValueError(f"no tile in {candidates} divides {n}")


def _rmsnorm(x, g, eps):
    return x * lax.rsqrt(jnp.mean(x * x, axis=-1, keepdims=True) + eps) * g


def _dot(a, b):
    return jnp.dot(a, b, preferred_element_type=_F32)


def _dot_nt(a, b):
    return lax.dot_general(a, b, (((1,), (1,)), ((), ())), preferred_element_type=_F32)


def _params(*semantics):
    return pltpu.CompilerParams(dimension_semantics=semantics,
                                vmem_limit_bytes=V7X_VMEM_LIMIT_BYTES)


def _resident(block_shape, index_map):
    return pl.BlockSpec(block_shape, index_map, pipeline_mode=pl.Buffered(1))


def _in_proj_kernel(x_ref, g_ref, w_ref, glu_ref, q_ref, k_ref, v_ref, kb_ref, vb_ref):
    d = x_ref.shape[1]
    h = _rmsnorm(x_ref[...], g_ref[...], RMS_EPS).astype(_BF16)

    def proj(c):
        return _dot(h, w_ref[:, c * d:(c + 1) * d])

    glu_ref[...] = proj(0) * jax.nn.sigmoid(proj(1))
    q_ref[...] = (proj(2) * (HEAD_DIM ** -0.5)).astype(_BF16)
    k = proj(3)
    k_ref[...] = k
    kb_ref[...] = k.astype(_BF16)
    v = proj(4)
    v_ref[...] = v
    vb_ref[...] = v.astype(_BF16)


def _in_proj(x, norm_g, w_in_b, layer):
    r, d = x.shape
    tm = _pick_tile(r, (640, 512, 320, 256, 128, 64, 32, 16))
    row = lambda i: (i, 0)
    f32_out = jax.ShapeDtypeStruct((r, d), _F32)
    bf16_out = jax.ShapeDtypeStruct((r, d), _BF16)
    return pl.pallas_call(
        _in_proj_kernel,
        grid=(r // tm,),
        in_specs=[
            pl.BlockSpec((tm, d), row),
            _resident((None, 1, d), lambda i: (layer, 0, 0)),
            _resident((None, d, 5 * d), lambda i: (layer, 0, 0)),
        ],
        out_specs=[pl.BlockSpec((tm, d), row)] * 6,
        out_shape=[f32_out, bf16_out, f32_out, f32_out, bf16_out, bf16_out],
        compiler_params=_params("parallel"),
        name="in_proj",
    )(x, norm_g, w_in_b)


def _conv_kernel(halo_ref, glu_ref, w_ref, b_ref, lng_ref, lnb_ref, c_ref, pad_ref, dw_ref,
                 *, zero_first_halo):
    tm, dc = glu_ref.shape
    halo = halo_ref[...]
    if zero_first_halo:
        halo = jnp.where(pl.program_id(0) == 0, jnp.zeros_like(halo), halo)
    pad_ref[0:HALO_ROWS, :] = halo
    pad_ref[HALO_ROWS:, :] = glu_ref[...]

    rows = 8
    lanes = 512
    first = HALO_ROWS - CONV_STATE
    for r0 in range(0, tm, rows):
        for c0 in range(0, dc, lanes):
            win = pad_ref[r0:r0 + rows + HALO_ROWS, c0:c0 + lanes]
            acc = jnp.broadcast_to(b_ref[:, c0:c0 + lanes], (rows, lanes))
            for j in range(CONV_WIDTH):
                acc = acc + w_ref[j:j + 1, c0:c0 + lanes] * win[first + j:first + j + rows, :]
            dw_ref[r0:r0 + rows, c0:c0 + lanes] = acc

    dw = dw_ref[...]
    mu = jnp.mean(dw, axis=-1, keepdims=True)
    cen = dw - mu
    var = jnp.mean(cen * cen, axis=-1, keepdims=True)
    y = cen * lax.rsqrt(var + LN_EPS) * lng_ref[...] + lnb_ref[...]
    c_ref[...] = (y * jax.nn.sigmoid(y)).astype(_BF16)


def _conv_branch(glu, halo_src, tm, n_tiles, row_block0, halo_map, zero_first_halo,
                 conv_dw, conv_dw_b, conv_ln_g, conv_ln_b, layer):
    dc = glu.shape[1]
    vec = _resident((None, 1, dc), lambda i: (layer, 0, 0))
    return pl.pallas_call(
        functools.partial(_conv_kernel, zero_first_halo=zero_first_halo),
        grid=(n_tiles,),
        in_specs=[
            pl.BlockSpec((HALO_ROWS, dc), halo_map),
            pl.BlockSpec((tm, dc), lambda i: (row_block0 + i, 0)),
            _resident((None, CONV_WIDTH, dc), lambda i: (layer, 0, 0)),
            vec, vec, vec,
        ],
        out_specs=pl.BlockSpec((tm, dc), lambda i: (i, 0)),
        out_shape=jax.ShapeDtypeStruct((n_tiles * tm, dc), _BF16),
        scratch_shapes=[pltpu.VMEM((tm + HALO_ROWS, dc), _F32), pltpu.VMEM((tm, dc), _F32)],
        compiler_params=_params("parallel"),
        name="conv_branch",
    )(halo_src, glu, conv_dw, conv_dw_b, conv_ln_g, conv_ln_b)


def _t5_bucket(rel):
    half = N_BUCKETS // 2
    max_exact = half // 2
    ret = jnp.where(rel > 0, half, 0)
    n = jnp.abs(rel)
    nf = jnp.maximum(n, 1).astype(jnp.float32)
    large = max_exact + (jnp.log(nf / max_exact) / math.log(MAX_DISTANCE / max_exact)
                         * (half - max_exact)).astype(jnp.int32)
    large = jnp.minimum(large, half - 1)
    return ret + jnp.where(n < max_exact, n, large)


def _rel_bias_tile(rel_bias, q_pos, k_pos):
    bucket = _t5_bucket(k_pos[None, :] - q_pos[:, None])
    return jnp.transpose(rel_bias.astype(_F32)[bucket], (2, 0, 1))


def _lambda(lq1_ref, lk1_ref, lq2_ref, lk2_ref, lambda_init):
    s1 = jnp.sum(lq1_ref[...] * lk1_ref[...], axis=-1, keepdims=True)
    s2 = jnp.sum(lq2_ref[...] * lk2_ref[...], axis=-1, keepdims=True)
    return jnp.exp(s1) - jnp.exp(s2) + lambda_init


def _split_maps(q):
    lane = lax.broadcasted_iota(jnp.int32, q.shape, 1)
    zero = jnp.zeros_like(q)
    return jnp.concatenate([jnp.where(lane < HEAD_DIM, q, zero),
                            jnp.where(lane >= HEAD_DIM, q, zero)], axis=0)


def _diff_combine(o, lam, g, lambda_init):
    tq = o.shape[0] // 2
    d = o[:tq] - lam * o[tq:]
    return (_rmsnorm(d, g, SUBLN_EPS) * (1.0 - lambda_init)).astype(_BF16)


def _prompt_attn_kernel(cfar_ref, lq1_ref, lk1_ref, lq2_ref, lk2_ref, q_ref, k_ref, v_ref,
                        bias_ref, g_ref, o_ref, m_ref, l_ref, acc_ref, *, lambda_init):
    bq = q_ref.shape[0]
    h = pl.program_id(0)
    i = pl.program_id(1)
    qq = _split_maps(q_ref[...])

    m_ref[...] = jnp.full(m_ref.shape, -jnp.inf, _F32)
    l_ref[...] = jnp.zeros(l_ref.shape, _F32)
    acc_ref[...] = jnp.zeros(acc_ref.shape, _F32)

    def update(j, s, shift):
        m_old = m_ref[...]
        m_new = jnp.maximum(m_old, jnp.max(s, axis=-1, keepdims=True) + shift)
        p = jnp.exp(s - (m_new - shift))
        alpha = jnp.exp(m_old - m_new)
        l_ref[...] = alpha * l_ref[...] + jnp.sum(p, axis=-1, keepdims=True)
        v = v_ref[pl.ds(pl.multiple_of(j * bq, bq), bq), :]
        acc_ref[...] = alpha * acc_ref[...] + _dot(p.astype(_BF16), v)
        m_ref[...] = m_new

    def scores(j):
        return _dot_nt(qq, k_ref[pl.ds(pl.multiple_of(j * bq, bq), bq), :])

    cfar = cfar_ref[h]

    def far_body(j, carry):
        update(j, scores(j), cfar)
        return carry

    lax.fori_loop(0, jnp.maximum(i - 1, 0), far_body, 0)

    @pl.when(i >= 1)
    def _():
        b = bias_ref[0]
        update(i - 1, scores(i - 1) + jnp.concatenate([b, b], axis=0), 0.0)

    b = bias_ref[1]
    qc = lax.broadcasted_iota(jnp.int32, b.shape, 0) // CHUNK
    kc = lax.broadcasted_iota(jnp.int32, b.shape, 1) // CHUNK
    allowed = kc <= qc
    s = scores(i) + jnp.concatenate([b, b], axis=0)
    s = jnp.where(jnp.concatenate([allowed, allowed], axis=0), s, NEG_INF)
    update(i, s, 0.0)

    lam = _lambda(lq1_ref, lk1_ref, lq2_ref, lk2_ref, lambda_init)
    o_ref[...] = _diff_combine(acc_ref[...] / l_ref[...], lam, g_ref[...], lambda_init)


def _prompt_attention(q, kb, vb, tp, rel_bias, lam_vecs, subln_g, layer, lambda_init):
    d_attn = q.shape[1]
    n_heads = d_attn // HEAD_COLS
    bq = _pick_tile(tp, (512, 256, 128))
    assert bq >= MAX_DISTANCE and bq % CHUNK == 0
    nq = tp // bq
    pos = jnp.arange(bq, dtype=jnp.int32)
    bias = jnp.stack([_rel_bias_tile(rel_bias, pos + bq, pos),
                      _rel_bias_tile(rel_bias, pos, pos)], axis=1)
    cfar = rel_bias.astype(_F32)[_t5_bucket(jnp.int32(-(bq + 1)))]
    lam_spec = _resident((None, 1, HEAD_DIM), lambda h, i: (layer, 0, 0))
    return pl.pallas_call(
        functools.partial(_prompt_attn_kernel, lambda_init=lambda_init),
        grid=(n_heads, nq),
        in_specs=[
            pl.BlockSpec(memory_space=pltpu.SMEM),
            lam_spec, lam_spec, lam_spec, lam_spec,
            pl.BlockSpec((bq, HEAD_COLS), lambda h, i: (i, h)),
            pl.BlockSpec((tp, HEAD_COLS), lambda h, i: (0, h)),
            pl.BlockSpec((tp, HEAD_COLS), lambda h, i: (0, h)),
            pl.BlockSpec((None, 2, bq, bq), lambda h, i: (h, 0, 0, 0)),
            _resident((None, 1, HEAD_COLS), lambda h, i: (layer, 0, 0)),
        ],
        out_specs=pl.BlockSpec((bq, HEAD_COLS), lambda h, i: (i, h)),
        out_shape=jax.ShapeDtypeStruct((tp, d_attn), _BF16),
        scratch_shapes=[pltpu.VMEM((2 * bq, 1), _F32), pltpu.VMEM((2 * bq, 1), _F32),
                        pltpu.VMEM((2 * bq, HEAD_COLS), _F32)],
        compiler_params=_params("parallel", "arbitrary"),
        name="prompt_attention",
    )(cfar, *lam_vecs, q, kb, vb, bias, subln_g)


def _sample_attn_kernel(lq1_ref, lk1_ref, lq2_ref, lk2_ref, q_ref, kn_ref, vn_ref, kc_ref, vc_ref,
                        bias_c_ref, bias_n_ref, g_ref, o_ref, *, lambda_init):
    qq = _split_maps(q_ref[...])
    bc = bias_c_ref[...]
    bn = bias_n_ref[...]
    s_c = _dot_nt(qq, kc_ref[...].astype(_BF16)) + jnp.concatenate([bc, bc], axis=0)
    s_n = _dot_nt(qq, kn_ref[...]) + jnp.concatenate([bn, bn], axis=0)
    m = jnp.maximum(jnp.max(s_c, axis=-1, keepdims=True), jnp.max(s_n, axis=-1, keepdims=True))
    p_c = jnp.exp(s_c - m)
    p_n = jnp.exp(s_n - m)
    l = jnp.sum(p_c, axis=-1, keepdims=True) + jnp.sum(p_n, axis=-1, keepdims=True)
    acc = _dot(p_c.astype(_BF16), vc_ref[...].astype(_BF16)) + _dot(p_n.astype(_BF16), vn_ref[...])
    lam = _lambda(lq1_ref, lk1_ref, lq2_ref, lk2_ref, lambda_init)
    o_ref[...] = _diff_combine(acc / l, lam, g_ref[...], lambda_init)


def _sample_attention(q, kb, vb, cache_k, cache_v, tp, bs, ts, rel_bias, lam_vecs, subln_g,
                      layer, lambda_init):
    d_attn = q.shape[1]
    n_heads = d_attn // HEAD_COLS
    past = cache_k.shape[2]
    assert tp % ts == 0 and ts % 16 == 0
    assert (past + ts - 1) // CHUNK <= past // CHUNK, "new frames must share the queries' chunk"
    row0 = tp // ts
    q_pos = past + jnp.arange(ts, dtype=jnp.int32)
    bias_c = _rel_bias_tile(rel_bias, q_pos, jnp.arange(past, dtype=jnp.int32))
    bias_n = _rel_bias_tile(rel_bias, q_pos, q_pos)
    new_rows = pl.BlockSpec((ts, HEAD_COLS), lambda b, h: (row0 + b, h))
    cache = pl.BlockSpec((None, None, past, HEAD_COLS), lambda b, h: (layer, b, 0, h))
    lam_spec = _resident((None, 1, HEAD_DIM), lambda b, h: (layer, 0, 0))
    return pl.pallas_call(
        functools.partial(_sample_attn_kernel, lambda_init=lambda_init),
        grid=(bs, n_heads),
        in_specs=[
            lam_spec, lam_spec, lam_spec, lam_spec,
            new_rows, new_rows, new_rows, cache, cache,
            pl.BlockSpec((None, ts, past), lambda b, h: (h, 0, 0)),
            pl.BlockSpec((None, ts, ts), lambda b, h: (h, 0, 0)),
            _resident((None, 1, HEAD_COLS), lambda b, h: (layer, 0, 0)),
        ],
        out_specs=pl.BlockSpec((ts, HEAD_COLS), lambda b, h: (b, h)),
        out_shape=jax.ShapeDtypeStruct((bs * ts, d_attn), _BF16),
        compiler_params=_params("parallel", "arbitrary"),
        name="sample_attention",
    )(*lam_vecs, q, kb, vb, cache_k, cache_v, bias_c, bias_n, subln_g)


def _merge_kernel(x_ref, c_ref, on_ref, g_ref, wga_ref, wgb_ref, wco_ref, wao_ref, wout_ref, y_ref):
    x = x_ref[...]
    h = _rmsnorm(x, g_ref[...], RMS_EPS).astype(_BF16)
    gate_a = jax.nn.sigmoid(_dot(h, wga_ref[...]))
    gate_b = jax.nn.sigmoid(_dot(h, wgb_ref[...]))
    merged = gate_a * _dot(c_ref[...], wco_ref[...]) + gate_b * _dot(on_ref[...], wao_ref[...])
    y_ref[...] = x + _dot(merged.astype(_BF16), wout_ref[...])


def _merge(x, c, o_n, norm_g, w_in_b, w_conv_out_b, w_attn_o_b, w_out_b, layer):
    r, d = x.shape
    tm = _pick_tile(r, (640, 512, 320, 256, 128, 64, 32, 16))
    gate_col0 = w_in_b.shape[2] // d - 2
    row = pl.BlockSpec((tm, d), lambda i: (i, 0))
    sq = _resident((None, d, d), lambda i: (layer, 0, 0))
    return pl.pallas_call(
        _merge_kernel,
        grid=(r // tm,),
        in_specs=[
            row, row, row,
            _resident((None, 1, d), lambda i: (layer, 0, 0)),
            _resident((None, d, d), lambda i: (layer, 0, gate_col0)),
            _resident((None, d, d), lambda i: (layer, 0, gate_col0 + 1)),
            sq, sq, sq,
        ],
        out_specs=row,
        out_shape=jax.ShapeDtypeStruct((r, d), _F32),
        compiler_params=_params("parallel"),
        name="merge_out_proj",
    )(x, c, o_n, norm_g, w_in_b, w_in_b, w_conv_out_b, w_attn_o_b, w_out_b)


def _ffn_kernel(x_ref, g_ref, wfi_ref, wfo_ref, gf_ref, y_ref, *, final_norm):
    d_ff = wfo_ref.shape[0]
    x = x_ref[...]
    h = _rmsnorm(x, g_ref[...], RMS_EPS).astype(_BF16)
    y = x
    half = (d_ff // 2) // 256 * 256
    for c0, c1 in ((0, half), (half, d_ff)):
        gate = _dot(h, wfi_ref[:, c0:c1])
        up = _dot(h, wfi_ref[:, d_ff + c0:d_ff + c1])
        act = (gate * jax.nn.sigmoid(gate) * up).astype(_BF16)
        y = y + _dot(act, wfo_ref[c0:c1, :])
    if final_norm:
        y = _rmsnorm(y, gf_ref[...], RMS_EPS)
    y_ref[...] = y


def _ffn(x, norm_g, w_ffn_in_b, w_ffn_out_b, norm_final, layer, final_norm):
    r, d = x.shape
    d_ff = w_ffn_out_b.shape[1]
    tm = _pick_tile(r, (640, 512, 320, 256, 128, 64, 32, 16))
    row = pl.BlockSpec((tm, d), lambda i: (i, 0))
    return pl.pallas_call(
        functools.partial(_ffn_kernel, final_norm=final_norm),
        grid=(r // tm,),
        in_specs=[
            row,
            _resident((None, 1, d), lambda i: (layer, 0, 0)),
            _resident((None, d, 2 * d_ff), lambda i: (layer, 0, 0)),
            _resident((None, d_ff, d), lambda i: (layer, 0, 0)),
            _resident((1, d), lambda i: (0, 0)),
        ],
        out_specs=row,
        out_shape=jax.ShapeDtypeStruct((r, d), _F32),
        compiler_params=_params("parallel"),
        name="ffn",
    )(x, norm_g, w_ffn_in_b, w_ffn_out_b, norm_final)


def kernel(x_prompt, x_sample, cache_k, cache_v, state_conv, norm_mix, w_in, conv_dw, conv_dw_b,
           conv_ln_g, conv_ln_b, w_conv_out, lambda_q1, lambda_k1, lambda_q2, lambda_k2,
           attn_subln_g, w_attn_o, w_out, norm_ffn, w_ffn_in, w_ffn_out, rel_bias, norm_final):
    depth = w_in.shape[0]
    bp, tp, d = x_prompt.shape
    bs, ts, _ = x_sample.shape
    past = cache_k.shape[2]
    n_heads = cache_k.shape[3]
    assert bp == 1, "one new stream per step"
    assert n_heads == N_HEADS and cache_k.shape[4] == HEAD_COLS
    assert ts >= CONV_STATE and ts % HALO_ROWS == 0 and tp % ts == 0
    dc = conv_dw.shape[2]

    x = jnp.concatenate([x_prompt.reshape(tp, d), x_sample.reshape(bs * ts, d)], axis=0)

    as_rows = lambda a: a.reshape(depth, 1, a.shape[-1])
    norm_mix3, norm_ffn3 = as_rows(norm_mix), as_rows(norm_ffn)
    conv_b3, ln_g3, ln_b3 = as_rows(conv_dw_b), as_rows(conv_ln_g), as_rows(conv_ln_b)
    subln3 = as_rows(attn_subln_g)
    lam3 = [as_rows(a) for a in (lambda_q1, lambda_k1, lambda_q2, lambda_k2)]
    norm_final2 = norm_final.reshape(1, d)
    w_in_b = w_in.astype(_BF16)
    w_conv_out_b = w_conv_out.astype(_BF16)
    w_attn_o_b = w_attn_o.astype(_BF16)
    w_out_b = w_out.astype(_BF16)
    w_ffn_in_b = w_ffn_in.astype(_BF16)
    w_ffn_out_b = w_ffn_out.astype(_BF16)
    cache_k4 = cache_k.reshape(depth, bs, past, n_heads * HEAD_COLS)
    cache_v4 = cache_v.reshape(depth, bs, past, n_heads * HEAD_COLS)
    halo_s = jnp.pad(state_conv, ((0, 0), (0, 0), (HALO_ROWS - CONV_STATE, 0), (0, 0)))
    halo_s = halo_s.reshape(depth, bs * HALO_ROWS, dc)

    tm_conv = _pick_tile(tp, (128, 64, 32))
    halo_per_tile = tm_conv // HALO_ROWS

    k_p, v_p, c_p, k_s, v_s, c_s = [], [], [], [], [], []
    for layer in range(depth):
        lambda_init = 0.8 - 0.6 * math.exp(-0.3 * layer)
        glu, q, k, v, kb, vb = _in_proj(x, norm_mix3, w_in_b, layer)

        conv_args = (conv_dw, conv_b3, ln_g3, ln_b3, layer)
        c_prompt = _conv_branch(
            glu, glu, tm_conv, tp // tm_conv, 0,
            lambda i: (jnp.maximum(i * halo_per_tile - 1, 0), 0), True, *conv_args)
        c_sample = _conv_branch(
            glu, halo_s[layer], ts, bs, tp // ts, lambda i: (i, 0), False, *conv_args)
        c = jnp.concatenate([c_prompt, c_sample], axis=0)

        o_prompt = _prompt_attention(q, kb, vb, tp, rel_bias, lam3, subln3, layer, lambda_init)
        o_sample = _sample_attention(q, kb, vb, cache_k4, cache_v4, tp, bs, ts, rel_bias, lam3,
                                     subln3, layer, lambda_init)
        o_n = jnp.concatenate([o_prompt, o_sample], axis=0)

        x = _merge(x, c, o_n, norm_mix3, w_in_b, w_conv_out_b, w_attn_o_b, w_out_b, layer)
        x = _ffn(x, norm_ffn3, w_ffn_in_b, w_ffn_out_b, norm_final2, layer, layer == depth - 1)

        k_p.append(k[:tp].reshape(1, tp, n_heads, HEAD_COLS))
        v_p.append(v[:tp].reshape(1, tp, n_heads, HEAD_COLS))
        c_p.append(glu[tp - CONV_STATE:tp].reshape(1, CONV_STATE, dc))
        k_s.append(k[tp:].reshape(bs, ts, n_heads, HEAD_COLS))
        v_s.append(v[tp:].reshape(bs, ts, n_heads, HEAD_COLS))
        c_s.append(glu[tp:].reshape(bs, ts, dc)[:, ts - CONV_STATE:])

    y_prompt = x[:tp].reshape(1, tp, d)
    y_sample = x[tp:].reshape(bs, ts, d)
    return (y_prompt, y_sample, jnp.stack(k_p), jnp.stack(v_p), jnp.stack(c_p),
            jnp.stack(k_s), jnp.stack(v_s), jnp.stack(c_s))
```

```python
import functools
import math

import jax
import jax.numpy as jnp
from jax import lax
from jax.experimental import pallas as pl
from jax.experimental.pallas import tpu as pltpu

N_HEADS = 8
HEAD_DIM = 64
HEAD_COLS = 2 * HEAD_DIM
CONV_WIDTH = 31
CONV_STATE = CONV_WIDTH - 1
CHUNK = 64
N_BUCKETS = 32
MAX_DISTANCE = 128
FAR_BUCKET = N_BUCKETS // 2 - 1
RMS_EPS = 1e-6
LN_EPS = 1e-5
SUBLN_EPS = 1e-5
NEG_INF = -1e30
LOG2E = math.log2(math.e)

HALO_ROWS = 32
SCORE_COLS = 256
V7X_VMEM_LIMIT_BYTES = 56 * 1024 * 1024

_F32 = jnp.float32
_BF16 = jnp.bfloat16


def _pick_tile(n, candidates):
    for c in candidates:
        if n % c == 0:
            return c
    raise ValueError(f"no tile in {candidates} divides {n}")


def _rmsnorm(x, g, eps):
    return x * lax.rsqrt(jnp.mean(x * x, axis=-1, keepdims=True) + eps) * g


def _dot(a, b):
    return jnp.dot(a, b, preferred_element_type=_F32)


def _dot_nt(a, b):
    return lax.dot_general(a, b, (((1,), (1,)), ((), ())), preferred_element_type=_F32)


def _params(*semantics):
    return pltpu.CompilerParams(dimension_semantics=semantics,
                                vmem_limit_bytes=V7X_VMEM_LIMIT_BYTES)


def _resident(block_shape, index_map):
    return pl.BlockSpec(block_shape, index_map, pipeline_mode=pl.Buffered(1))


def _in_proj_kernel(x_ref, g_ref, w_ref, glu_ref, q_ref, k_ref, v_ref, kb_ref, vb_ref):
    d = x_ref.shape[1]
    h = _rmsnorm(x_ref[...], g_ref[...], RMS_EPS).astype(_BF16)

    def proj(c):
        return _dot(h, w_ref[:, c * d:(c + 1) * d])

    glu_ref[...] = proj(0) * jax.nn.sigmoid(proj(1))
    q_ref[...] = (proj(2) * (HEAD_DIM ** -0.5 * LOG2E)).astype(_BF16)
    k = proj(3)
    k_ref[...] = k
    kb_ref[...] = k.astype(_BF16)
    v = proj(4)
    v_ref[...] = v
    vb_ref[...] = v.astype(_BF16)


def _in_proj(x, norm_g, w_in_b, layer):
    r, d = x.shape
    tm = _pick_tile(r, (640, 512, 320, 256, 128, 64, 32, 16))
    row = lambda i: (i, 0)
    f32_out = jax.ShapeDtypeStruct((r, d), _F32)
    bf16_out = jax.ShapeDtypeStruct((r, d), _BF16)
    return pl.pallas_call(
        _in_proj_kernel,
        grid=(r // tm,),
        in_specs=[
            pl.BlockSpec((tm, d), row),
            _resident((None, 1, d), lambda i: (layer, 0, 0)),
            _resident((None, d, 5 * d), lambda i: (layer, 0, 0)),
        ],
        out_specs=[pl.BlockSpec((tm, d), row)] * 6,
        out_shape=[f32_out, bf16_out, f32_out, f32_out, bf16_out, bf16_out],
        compiler_params=_params("parallel"),
        name="in_proj",
    )(x, norm_g, w_in_b)


def _conv_kernel(halo_ref, glu_ref, w_ref, b_ref, lng_ref, lnb_ref, c_ref, pad_ref, dw_ref,
                 *, zero_first_halo):
    tm, dc = glu_ref.shape
    halo = halo_ref[...]
    if zero_first_halo:
        halo = jnp.where(pl.program_id(0) == 0, jnp.zeros_like(halo), halo)
    pad_ref[0:HALO_ROWS, :] = halo
    pad_ref[HALO_ROWS:, :] = glu_ref[...]

    rows = 8
    lanes = 512
    first = HALO_ROWS - CONV_STATE
    for r0 in range(0, tm, rows):
        for c0 in range(0, dc, lanes):
            win = pad_ref[r0:r0 + rows + HALO_ROWS, c0:c0 + lanes]
            acc = jnp.broadcast_to(b_ref[:, c0:c0 + lanes], (rows, lanes))
            for j in range(CONV_WIDTH):
                acc = acc + w_ref[j:j + 1, c0:c0 + lanes] * win[first + j:first + j + rows, :]
            dw_ref[r0:r0 + rows, c0:c0 + lanes] = acc

    dw = dw_ref[...]
    mu = jnp.mean(dw, axis=-1, keepdims=True)
    cen = dw - mu
    var = jnp.mean(cen * cen, axis=-1, keepdims=True)
    y = cen * lax.rsqrt(var + LN_EPS) * lng_ref[...] + lnb_ref[...]
    c_ref[...] = (y * jax.nn.sigmoid(y)).astype(_BF16)


def _conv_branch(glu, halo_src, tm, n_tiles, row_block0, halo_map, zero_first_halo,
                 conv_dw, conv_dw_b, conv_ln_g, conv_ln_b, layer):
    dc = glu.shape[1]
    vec = _resident((None, 1, dc), lambda i: (layer, 0, 0))
    return pl.pallas_call(
        functools.partial(_conv_kernel, zero_first_halo=zero_first_halo),
        grid=(n_tiles,),
        in_specs=[
            pl.BlockSpec((HALO_ROWS, dc), halo_map),
            pl.BlockSpec((tm, dc), lambda i: (row_block0 + i, 0)),
            _resident((None, CONV_WIDTH, dc), lambda i: (layer, 0, 0)),
            vec, vec, vec,
        ],
        out_specs=pl.BlockSpec((tm, dc), lambda i: (i, 0)),
        out_shape=jax.ShapeDtypeStruct((n_tiles * tm, dc), _BF16),
        scratch_shapes=[pltpu.VMEM((tm + HALO_ROWS, dc), _F32), pltpu.VMEM((tm, dc), _F32)],
        compiler_params=_params("parallel"),
        name="conv_branch",
    )(halo_src, glu, conv_dw, conv_dw_b, conv_ln_g, conv_ln_b)


def _t5_bucket(rel):
    half = N_BUCKETS // 2
    max_exact = half // 2
    ret = jnp.where(rel > 0, half, 0)
    n = jnp.abs(rel)
    nf = jnp.maximum(n, 1).astype(jnp.float32)
    large = max_exact + (jnp.log(nf / max_exact) / math.log(MAX_DISTANCE / max_exact)
                         * (half - max_exact)).astype(jnp.int32)
    large = jnp.minimum(large, half - 1)
    return ret + jnp.where(n < max_exact, n, large)


def _bias_from_buckets(bucket, rb_ref, h):
    out = jnp.zeros(bucket.shape, _F32)
    for b in range(N_BUCKETS):
        out = jnp.where(bucket == b, rb_ref[b, h] * LOG2E, out)
    return out


def _reduce_rows(x, op):
    slab = 8
    parts = [x[r:r + slab] for r in range(0, x.shape[0], slab)]
    while len(parts) > 1:
        nxt = [op(parts[a], parts[a + 1]) for a in range(0, len(parts) - 1, 2)]
        if len(parts) % 2:
            nxt.append(parts[-1])
        parts = nxt
    red = jnp.max if op is jnp.maximum else jnp.sum
    return red(parts[0], axis=0, keepdims=True)


def _lambda(lq1_ref, lk1_ref, lq2_ref, lk2_ref, lambda_init):
    s1 = jnp.sum(lq1_ref[...] * lk1_ref[...], axis=-1, keepdims=True)
    s2 = jnp.sum(lq2_ref[...] * lk2_ref[...], axis=-1, keepdims=True)
    return jnp.exp(s1) - jnp.exp(s2) + lambda_init


def _prompt_attn_kernel(rb_ref, lq1_ref, lk1_ref, lq2_ref, lk2_ref, q_ref, k_ref, v_ref,
                        bucket_ref, g_ref, o_ref,
                        qq_ref, vt_ref, bias_ref, s_ref, smax_ref, m_ref, l_ref, acc_ref,
                        *, lambda_init):
    bq = q_ref.shape[0]
    n_blocks = vt_ref.shape[0]
    h = pl.program_id(0)
    i = pl.program_id(1)

    @pl.when(i == 0)
    def _():
        for t in range(2):
            bias_ref[t] = _bias_from_buckets(bucket_ref[t], rb_ref, h)

        def transpose_block(jb, carry):
            rows = pl.ds(pl.multiple_of(jb * bq, bq), bq)
            vt_ref[jb] = v_ref[rows, :].astype(_F32).T.astype(_BF16)
            return carry

        lax.fori_loop(0, n_blocks, transpose_block, 0)

    qt = q_ref[...].astype(_F32).T.astype(_BF16)
    zeros = jnp.zeros((HEAD_DIM, bq), _BF16)
    qq_ref[0:HEAD_DIM, 0:bq] = qt[0:HEAD_DIM]
    qq_ref[HEAD_DIM:, 0:bq] = zeros
    qq_ref[0:HEAD_DIM, bq:] = zeros
    qq_ref[HEAD_DIM:, bq:] = qt[HEAD_DIM:]

    m_ref[...] = jnp.full(m_ref.shape, -jnp.inf, _F32)
    l_ref[...] = jnp.zeros(l_ref.shape, _F32)
    acc_ref[...] = jnp.zeros(acc_ref.shape, _F32)

    tiles = list(range(0, 2 * bq, SCORE_COLS))

    def key_rows(j):
        return k_ref[pl.ds(pl.multiple_of(j * bq, bq), bq), :]

    def issue_scores(n, k_blk):
        s = _dot(k_blk, qq_ref[:, tiles[n]:tiles[n] + SCORE_COLS])
        s_ref[n] = s
        smax_ref[n] = _reduce_rows(s, jnp.maximum)

    def block(j, shift, bias_tile, masked, next_j):
        vt = vt_ref[j]
        k_next = None if next_j is None else key_rows(next_j)
        deferred = None
        for n, c0 in enumerate(tiles):
            cols = slice(c0, c0 + SCORE_COLS)
            s = s_ref[n]
            if bias_tile is None:
                s_max = smax_ref[n]
            else:
                q0 = c0 % bq
                s = s + bias_ref[bias_tile, :, q0:q0 + SCORE_COLS]
                if masked:
                    kc = lax.broadcasted_iota(jnp.int32, s.shape, 0) // CHUNK
                    qc = (q0 + lax.broadcasted_iota(jnp.int32, s.shape, 1)) // CHUNK
                    s = jnp.where(kc <= qc, s, NEG_INF)
                s_max = _reduce_rows(s, jnp.maximum)
            m_old = m_ref[:, cols]
            m_new = jnp.maximum(m_old, s_max + shift)
            p = jnp.exp2(s - (m_new - shift))
            alpha = jnp.exp2(m_old - m_new)
            l_ref[:, cols] = alpha * l_ref[:, cols] + _reduce_rows(p, jnp.add)
            m_ref[:, cols] = m_new
            pv = _dot(vt, p.astype(_BF16))
            if k_next is not None:
                issue_scores(n, k_next)
            if deferred is not None:
                dcols, dalpha, dpv = deferred
                acc_ref[:, dcols] = dalpha * acc_ref[:, dcols] + dpv
            deferred = (cols, alpha, pv)
        dcols, dalpha, dpv = deferred
        acc_ref[:, dcols] = dalpha * acc_ref[:, dcols] + dpv

    k_first = key_rows(0)
    for n in range(len(tiles)):
        issue_scores(n, k_first)

    far_bias = rb_ref[FAR_BUCKET, h] * LOG2E

    n_far = jnp.maximum(i - 1, 0)

    def far_pair(t, carry):
        block(2 * t, far_bias, None, False, 2 * t + 1)
        block(2 * t + 1, far_bias, None, False, 2 * t + 2)
        return carry

    def far_single(t, carry):
        block(n_far - 1, far_bias, None, False, n_far)
        return carry

    lax.fori_loop(0, n_far // 2, far_pair, 0)
    lax.fori_loop(0, n_far % 2, far_single, 0)

    @pl.when(i >= 1)
    def _():
        block(i - 1, 0.0, 0, False, i)

    block(i, 0.0, 1, True, None)

    lam = _lambda(lq1_ref, lk1_ref, lq2_ref, lk2_ref, lambda_init)
    o = acc_ref[...] / l_ref[...]
    d = o[:, 0:bq] - lam * o[:, bq:]
    y = d * lax.rsqrt(jnp.mean(d * d, axis=0, keepdims=True) + SUBLN_EPS)
    o_ref[...] = (y.T * g_ref[...] * (1.0 - lambda_init)).astype(_BF16)


def _prompt_attention(q, kb, vb, tp, rel_bias, lam_vecs, subln_g, layer, lambda_init):
    d_attn = q.shape[1]
    n_heads = d_attn // HEAD_COLS
    bq = _pick_tile(tp, (512, 256))
    assert bq + 1 >= MAX_DISTANCE and bq % CHUNK == 0 and bq % SCORE_COLS == 0
    nq = tp // bq
    pos = jnp.arange(bq, dtype=jnp.int32)
    bucket = jnp.stack([_t5_bucket(pos[:, None] - (pos[None, :] + bq)),
                        _t5_bucket(pos[:, None] - pos[None, :])])
    lam_spec = _resident((None, 1, HEAD_DIM), lambda h, i: (layer, 0, 0))
    return pl.pallas_call(
        functools.partial(_prompt_attn_kernel, lambda_init=lambda_init),
        grid=(n_heads, nq),
        in_specs=[
            pl.BlockSpec(memory_space=pltpu.SMEM),
            lam_spec, lam_spec, lam_spec, lam_spec,
            pl.BlockSpec((bq, HEAD_COLS), lambda h, i: (i, h)),
            pl.BlockSpec((tp, HEAD_COLS), lambda h, i: (0, h)),
            pl.BlockSpec((tp, HEAD_COLS), lambda h, i: (0, h)),
            _resident((2, bq, bq), lambda h, i: (0, 0, 0)),
            _resident((None, 1, HEAD_COLS), lambda h, i: (layer, 0, 0)),
        ],
        out_specs=pl.BlockSpec((bq, HEAD_COLS), lambda h, i: (i, h)),
        out_shape=jax.ShapeDtypeStruct((tp, d_attn), _BF16),
        scratch_shapes=[
            pltpu.VMEM((HEAD_COLS, 2 * bq), _BF16),
            pltpu.VMEM((nq, HEAD_COLS, bq), _BF16),
            pltpu.VMEM((2, bq, bq), _F32),
            pltpu.VMEM((2 * bq // SCORE_COLS, bq, SCORE_COLS), _F32),
            pltpu.VMEM((2 * bq // SCORE_COLS, 1, SCORE_COLS), _F32),
            pltpu.VMEM((1, 2 * bq), _F32),
            pltpu.VMEM((1, 2 * bq), _F32),
            pltpu.VMEM((HEAD_COLS, 2 * bq), _F32),
        ],
        compiler_params=_params("arbitrary", "arbitrary"),
        name="prompt_attention",
    )(rel_bias, *lam_vecs, q, kb, vb, bucket, subln_g)


def _sample_attn_kernel(rb_ref, lq1_ref, lk1_ref, lq2_ref, lk2_ref, q_ref, kn_ref, vn_ref,
                        kc_ref, vc_ref, bucket_c_ref, bucket_n_ref, g_ref, o_ref,
                        bias_c_ref, bias_n_ref, *, lambda_init):
    ts = q_ref.shape[0]
    h = pl.program_id(0)

    @pl.when(pl.program_id(1) == 0)
    def _():
        bias_c_ref[...] = _bias_from_buckets(bucket_c_ref[...], rb_ref, h)
        bias_n_ref[...] = _bias_from_buckets(bucket_n_ref[...], rb_ref, h)

    q = q_ref[...]
    lane = lax.broadcasted_iota(jnp.int32, q.shape, 1)
    zero = jnp.zeros_like(q)
    qq = jnp.concatenate([jnp.where(lane < HEAD_DIM, q, zero),
                          jnp.where(lane >= HEAD_DIM, q, zero)], axis=0)
    bc = bias_c_ref[...]
    bn = bias_n_ref[...]
    s_c = _dot_nt(qq, kc_ref[...].astype(_BF16)) + jnp.concatenate([bc, bc], axis=0)
    s_n = _dot_nt(qq, kn_ref[...].astype(_BF16)) + jnp.concatenate([bn, bn], axis=0)
    m = jnp.maximum(jnp.max(s_c, axis=-1, keepdims=True), jnp.max(s_n, axis=-1, keepdims=True))
    p_c = jnp.exp2(s_c - m)
    p_n = jnp.exp2(s_n - m)
    l = jnp.sum(p_c, axis=-1, keepdims=True) + jnp.sum(p_n, axis=-1, keepdims=True)
    acc = (_dot(p_c.astype(_BF16), vc_ref[...].astype(_BF16))
           + _dot(p_n.astype(_BF16), vn_ref[...].astype(_BF16)))
    o = acc / l
    lam = _lambda(lq1_ref, lk1_ref, lq2_ref, lk2_ref, lambda_init)
    d = o[:ts] - lam * o[ts:]
    o_ref[...] = (_rmsnorm(d, g_ref[...], SUBLN_EPS) * (1.0 - lambda_init)).astype(_BF16)


def _sample_attention(q, k, v, cache_k, cache_v, tp, bs, ts, rel_bias, lam_vecs, subln_g,
                      layer, lambda_init):
    d_attn = q.shape[1]
    n_heads = d_attn // HEAD_COLS
    past = cache_k.shape[2]
    assert tp % ts == 0 and ts % 16 == 0
    assert (past + ts - 1) // CHUNK <= past // CHUNK, "new frames must share the queries' chunk"
    row0 = tp // ts
    q_pos = past + jnp.arange(ts, dtype=jnp.int32)
    bucket_c = _t5_bucket(jnp.arange(past, dtype=jnp.int32)[None, :] - q_pos[:, None])
    bucket_n = _t5_bucket(q_pos[None, :] - q_pos[:, None])
    new_rows = pl.BlockSpec((ts, HEAD_COLS), lambda h, b: (row0 + b, h))
    cache = pl.BlockSpec((None, None, past, HEAD_COLS), lambda h, b: (layer, b, 0, h))
    lam_spec = _resident((None, 1, HEAD_DIM), lambda h, b: (layer, 0, 0))
    return pl.pallas_call(
        functools.partial(_sample_attn_kernel, lambda_init=lambda_init),
        grid=(n_heads, bs),
        in_specs=[
            pl.BlockSpec(memory_space=pltpu.SMEM),
            lam_spec, lam_spec, lam_spec, lam_spec,
            new_rows, new_rows, new_rows, cache, cache,
            _resident((ts, past), lambda h, b: (0, 0)),
            _resident((ts, ts), lambda h, b: (0, 0)),
            _resident((None, 1, HEAD_COLS), lambda h, b: (layer, 0, 0)),
        ],
        out_specs=pl.BlockSpec((ts, HEAD_COLS), lambda h, b: (b, h)),
        out_shape=jax.ShapeDtypeStruct((bs * ts, d_attn), _BF16),
        scratch_shapes=[pltpu.VMEM((ts, past), _F32), pltpu.VMEM((ts, ts), _F32)],
        compiler_params=_params("arbitrary", "arbitrary"),
        name="sample_attention",
    )(rel_bias, *lam_vecs, q, k, v, cache_k, cache_v, bucket_c, bucket_n, subln_g)


def _merge_kernel(x_ref, c_ref, on_ref, g_ref, wga_ref, wgb_ref, wco_ref, wao_ref, wout_ref, y_ref):
    x = x_ref[...]
    h = _rmsnorm(x, g_ref[...], RMS_EPS).astype(_BF16)
    gate_a = jax.nn.sigmoid(_dot(h, wga_ref[...]))
    gate_b = jax.nn.sigmoid(_dot(h, wgb_ref[...]))
    merged = gate_a * _dot(c_ref[...], wco_ref[...]) + gate_b * _dot(on_ref[...], wao_ref[...])
    y_ref[...] = x + _dot(merged.astype(_BF16), wout_ref[...])


def _merge(x, c, o_n, norm_g, w_in_b, w_conv_out_b, w_attn_o_b, w_out_b, layer):
    r, d = x.shape
    tm = _pick_tile(r, (640, 512, 320, 256, 128, 64, 32, 16))
    gate_col0 = w_in_b.shape[2] // d - 2
    row = pl.BlockSpec((tm, d), lambda i: (i, 0))
    sq = _resident((None, d, d), lambda i: (layer, 0, 0))
    return pl.pallas_call(
        _merge_kernel,
        grid=(r // tm,),
        in_specs=[
            row, row, row,
            _resident((None, 1, d), lambda i: (layer, 0, 0)),
            _resident((None, d, d), lambda i: (layer, 0, gate_col0)),
            _resident((None, d, d), lambda i: (layer, 0, gate_col0 + 1)),
            sq, sq, sq,
        ],
        out_specs=row,
        out_shape=jax.ShapeDtypeStruct((r, d), _F32),
        compiler_params=_params("parallel"),
        name="merge_out_proj",
    )(x, c, o_n, norm_g, w_in_b, w_in_b, w_conv_out_b, w_attn_o_b, w_out_b)


def _ffn_kernel(x_ref, g_ref, wfi_ref, wfo_ref, gf_ref, y_ref, *, final_norm):
    d_ff = wfo_ref.shape[0]
    x = x_ref[...]
    h = _rmsnorm(x, g_ref[...], RMS_EPS).astype(_BF16)
    y = x
    half = (d_ff // 2) // 256 * 256
    for c0, c1 in ((0, half), (half, d_ff)):
        gate = _dot(h, wfi_ref[:, c0:c1])
        up = _dot(h, wfi_ref[:, d_ff + c0:d_ff + c1])
        act = (gate * jax.nn.sigmoid(gate) * up).astype(_BF16)
        y = y + _dot(act, wfo_ref[c0:c1, :])
    if final_norm:
        y = _rmsnorm(y, gf_ref[...], RMS_EPS)
    y_ref[...] = y


def _ffn(x, norm_g, w_ffn_in_b, w_ffn_out_b, norm_final, layer, final_norm):
    r, d = x.shape
    d_ff = w_ffn_out_b.shape[1]
    tm = _pick_tile(r, (640, 512, 320, 256, 128, 64, 32, 16))
    row = pl.BlockSpec((tm, d), lambda i: (i, 0))
    return pl.pallas_call(
        functools.partial(_ffn_kernel, final_norm=final_norm),
        grid=(r // tm,),
        in_specs=[
            row,
            _resident((None, 1, d), lambda i: (layer, 0, 0)),
            _resident((None, d, 2 * d_ff), lambda i: (layer, 0, 0)),
            _resident((None, d_ff, d), lambda i: (layer, 0, 0)),
            _resident((1, d), lambda i: (0, 0)),
        ],
        out_specs=row,
        out_shape=jax.ShapeDtypeStruct((r, d), _F32),
        compiler_params=_params("parallel"),
        name="ffn",
    )(x, norm_g, w_ffn_in_b, w_ffn_out_b, norm_final)


def kernel(x_prompt, x_sample, cache_k, cache_v, state_conv, norm_mix, w_in, conv_dw, conv_dw_b,
           conv_ln_g, conv_ln_b, w_conv_out, lambda_q1, lambda_k1, lambda_q2, lambda_k2,
           attn_subln_g, w_attn_o, w_out, norm_ffn, w_ffn_in, w_ffn_out, rel_bias, norm_final):
    depth = w_in.shape[0]
    bp, tp, d = x_prompt.shape
    bs, ts, _ = x_sample.shape
    past = cache_k.shape[2]
    n_heads = cache_k.shape[3]
    assert bp == 1, "one new stream per step"
    assert n_heads == N_HEADS and cache_k.shape[4] == HEAD_COLS
    assert rel_bias.shape == (N_BUCKETS, N_HEADS)
    assert ts >= CONV_STATE and ts % HALO_ROWS == 0 and tp % ts == 0
    dc = conv_dw.shape[2]

    x = jnp.concatenate([x_prompt.reshape(tp, d), x_sample.reshape(bs * ts, d)], axis=0)

    as_rows = lambda a: a.reshape(depth, 1, a.shape[-1])
    norm_mix3, norm_ffn3 = as_rows(norm_mix), as_rows(norm_ffn)
    conv_b3, ln_g3, ln_b3 = as_rows(conv_dw_b), as_rows(conv_ln_g), as_rows(conv_ln_b)
    subln3 = as_rows(attn_subln_g)
    lam3 = [as_rows(a) for a in (lambda_q1, lambda_k1, lambda_q2, lambda_k2)]
    norm_final2 = norm_final.reshape(1, d)
    rel_bias = rel_bias.astype(_F32)
    w_in_b = w_in.astype(_BF16)
    w_conv_out_b = w_conv_out.astype(_BF16)
    w_attn_o_b = w_attn_o.astype(_BF16)
    w_out_b = w_out.astype(_BF16)
    w_ffn_in_b = w_ffn_in.astype(_BF16)
    w_ffn_out_b = w_ffn_out.astype(_BF16)
    cache_k4 = cache_k.reshape(depth, bs, past, n_heads * HEAD_COLS)
    cache_v4 = cache_v.reshape(depth, bs, past, n_heads * HEAD_COLS)
    halo_s = jnp.pad(state_conv, ((0, 0), (0, 0), (HALO_ROWS - CONV_STATE, 0), (0, 0)))
    halo_s = halo_s.reshape(depth, bs * HALO_ROWS, dc)

    tm_conv = _pick_tile(tp, (128, 64, 32))
    halo_per_tile = tm_conv // HALO_ROWS

    k_p, v_p, c_p, k_s, v_s, c_s = [], [], [], [], [], []
    for layer in range(depth):
        lambda_init = 0.8 - 0.6 * math.exp(-0.3 * layer)
        glu, q, k, v, kb, vb = _in_proj(x, norm_mix3, w_in_b, layer)

        conv_args = (conv_dw, conv_b3, ln_g3, ln_b3, layer)
        c_prompt = _conv_branch(
            glu, glu, tm_conv, tp // tm_conv, 0,
            lambda i: (jnp.maximum(i * halo_per_tile - 1, 0), 0), True, *conv_args)
        c_sample = _conv_branch(
            glu, halo_s[layer], ts, bs, tp // ts, lambda i: (i, 0), False, *conv_args)
        c = jnp.concatenate([c_prompt, c_sample], axis=0)

        o_prompt = _prompt_attention(q, kb, vb, tp, rel_bias, lam3, subln3, layer, lambda_init)
        o_sample = _sample_attention(q, k, v, cache_k4, cache_v4, tp, bs, ts, rel_bias, lam3,
                                     subln3, layer, lambda_init)
        o_n = jnp.concatenate([o_prompt, o_sample], axis=0)

        x = _merge(x, c, o_n, norm_mix3, w_in_b, w_conv_out_b, w_attn_o_b, w_out_b, layer)
        x = _ffn(x, norm_ffn3, w_ffn_in_b, w_ffn_out_b, norm_final2, layer, layer == depth - 1)

        k_p.append(k[:tp].reshape(1, tp, n_heads, HEAD_COLS))
        v_p.append(v[:tp].reshape(1, tp, n_heads, HEAD_COLS))
        c_p.append(glu[tp - CONV_STATE:tp].reshape(1, CONV_STATE, dc))
        k_s.append(k[tp:].reshape(bs, ts, n_heads, HEAD_COLS))
        v_s.append(v[tp:].reshape(bs, ts, n_heads, HEAD_COLS))
        c_s.append(glu[tp:].reshape(bs, ts, dc)[:, ts - CONV_STATE:])

    y_prompt = x[:tp].reshape(1, tp, d)
    y_sample = x[tp:].reshape(bs, ts, d)
    return (y_prompt, y_sample, jnp.stack(k_p), jnp.stack(v_p), jnp.stack(c_p),
            jnp.stack(k_s), jnp.stack(v_s), jnp.stack(c_s))
```

```python
import functools
import math

import jax
import jax.numpy as jnp
from jax import lax
from jax.experimental import pallas as pl
from jax.experimental.pallas import tpu as pltpu

N_HEADS = 8
HEAD_DIM = 64
HEAD_COLS = 2 * HEAD_DIM
CONV_WIDTH = 31
CONV_STATE = CONV_WIDTH - 1
CHUNK = 64
N_BUCKETS = 32
MAX_DISTANCE = 128
FAR_BUCKET = N_BUCKETS // 2 - 1
RMS_EPS = 1e-6
LN_EPS = 1e-5
SUBLN_EPS = 1e-5
LOG2E = math.log2(math.e)

HALO_ROWS = 32
SCORE_COLS = 256
FAR_UNROLL = 4
V7X_VMEM_LIMIT_BYTES = 56 * 1024 * 1024

_F32 = jnp.float32
_BF16 = jnp.bfloat16


def _pick_tile(n, candidates):
    for c in candidates:
        if n % c == 0:
            return c
    raise ValueError(f"no tile in {candidates} divides {n}")


def _rmsnorm(x, g, eps):
    return x * lax.rsqrt(jnp.mean(x * x, axis=-1, keepdims=True) + eps) * g


def _dot(a, b):
    return jnp.dot(a, b, preferred_element_type=_F32)


def _dot_nt(a, b):
    return lax.dot_general(a, b, (((1,), (1,)), ((), ())), preferred_element_type=_F32)


def _params(*semantics):
    return pltpu.CompilerParams(dimension_semantics=semantics,
                                vmem_limit_bytes=V7X_VMEM_LIMIT_BYTES)


def _resident(block_shape, index_map):
    return pl.BlockSpec(block_shape, index_map, pipeline_mode=pl.Buffered(1))


def _in_proj_kernel(x_ref, g_ref, w_ref, glu_ref, q_ref, k_ref, v_ref, kb_ref, vb_ref):
    d = x_ref.shape[1]
    h = _rmsnorm(x_ref[...], g_ref[...], RMS_EPS).astype(_BF16)

    def proj(c):
        return _dot(h, w_ref[:, c * d:(c + 1) * d])

    glu_ref[...] = proj(0) * jax.nn.sigmoid(proj(1))
    q_ref[...] = (proj(2) * (HEAD_DIM ** -0.5 * LOG2E)).astype(_BF16)
    k = proj(3)
    k_ref[...] = k
    kb_ref[...] = k.astype(_BF16)
    v = proj(4)
    v_ref[...] = v
    vb_ref[...] = v.astype(_BF16)


def _in_proj(x, norm_g, w_in_b, layer):
    r, d = x.shape
    tm = _pick_tile(r, (640, 512, 320, 256, 128, 64, 32, 16))
    row = lambda i: (i, 0)
    f32_out = jax.ShapeDtypeStruct((r, d), _F32)
    bf16_out = jax.ShapeDtypeStruct((r, d), _BF16)
    return pl.pallas_call(
        _in_proj_kernel,
        grid=(r // tm,),
        in_specs=[
            pl.BlockSpec((tm, d), row),
            _resident((None, 1, d), lambda i: (layer, 0, 0)),
            _resident((None, d, 5 * d), lambda i: (layer, 0, 0)),
        ],
        out_specs=[pl.BlockSpec((tm, d), row)] * 6,
        out_shape=[f32_out, bf16_out, f32_out, f32_out, bf16_out, bf16_out],
        compiler_params=_params("parallel"),
        name="in_proj",
    )(x, norm_g, w_in_b)


def _conv_kernel(halo_ref, glu_ref, w_ref, b_ref, lng_ref, lnb_ref, c_ref, pad_ref, dw_ref,
                 *, zero_first_halo):
    tm, dc = glu_ref.shape
    halo = halo_ref[...]
    if zero_first_halo:
        halo = jnp.where(pl.program_id(0) == 0, jnp.zeros_like(halo), halo)
    pad_ref[0:HALO_ROWS, :] = halo
    pad_ref[HALO_ROWS:, :] = glu_ref[...]

    first = HALO_ROWS - CONV_STATE
    rows = _pick_tile(tm, (64, 32))
    lanes = 128
    for r0 in range(0, tm, rows):
        for c0 in range(0, dc, lanes):
            cs = slice(c0, c0 + lanes)
            acc = jnp.broadcast_to(b_ref[:, cs], (rows, lanes))
            for b in range(8):
                ext = rows + (8 if b else 0)
                part = None
                for a in range((first + CONV_WIDTH - 1) // 8 + 1):
                    j = 8 * a + b - first
                    if 0 <= j < CONV_WIDTH:
                        term = w_ref[j:j + 1, cs] * pad_ref[r0 + 8 * a:r0 + 8 * a + ext, cs]
                        part = term if part is None else part + term
                if b:
                    part = pltpu.roll(part, ext - b, axis=0)[:rows]
                acc = acc + part
            dw_ref[r0:r0 + rows, cs] = acc

    dw = dw_ref[...]
    mu = jnp.mean(dw, axis=-1, keepdims=True)
    cen = dw - mu
    var = jnp.mean(cen * cen, axis=-1, keepdims=True)
    y = cen * lax.rsqrt(var + LN_EPS) * lng_ref[...] + lnb_ref[...]
    c_ref[...] = (y * jax.nn.sigmoid(y)).astype(_BF16)


def _conv_branch(glu, halo_src, tm, n_tiles, row_block0, halo_map, zero_first_halo,
                 conv_dw, conv_dw_b, conv_ln_g, conv_ln_b, layer):
    dc = glu.shape[1]
    vec = _resident((None, 1, dc), lambda i: (layer, 0, 0))
    return pl.pallas_call(
        functools.partial(_conv_kernel, zero_first_halo=zero_first_halo),
        grid=(n_tiles,),
        in_specs=[
            pl.BlockSpec((HALO_ROWS, dc), halo_map),
            pl.BlockSpec((tm, dc), lambda i: (row_block0 + i, 0)),
            _resident((None, CONV_WIDTH, dc), lambda i: (layer, 0, 0)),
            vec, vec, vec,
        ],
        out_specs=pl.BlockSpec((tm, dc), lambda i: (i, 0)),
        out_shape=jax.ShapeDtypeStruct((n_tiles * tm, dc), _BF16),
        scratch_shapes=[pltpu.VMEM((tm + HALO_ROWS, dc), _F32), pltpu.VMEM((tm, dc), _F32)],
        compiler_params=_params("parallel"),
        name="conv_branch",
    )(halo_src, glu, conv_dw, conv_dw_b, conv_ln_g, conv_ln_b)


def _t5_bucket(rel):
    half = N_BUCKETS // 2
    max_exact = half // 2
    ret = jnp.where(rel > 0, half, 0)
    n = jnp.abs(rel)
    nf = jnp.maximum(n, 1).astype(jnp.float32)
    large = max_exact + (jnp.log(nf / max_exact) / math.log(MAX_DISTANCE / max_exact)
                         * (half - max_exact)).astype(jnp.int32)
    large = jnp.minimum(large, half - 1)
    return ret + jnp.where(n < max_exact, n, large)


def _bias_from_buckets(bucket, rb_ref, h):
    out = jnp.zeros(bucket.shape, _F32)
    for b in range(N_BUCKETS):
        out = jnp.where(bucket == b, rb_ref[b, h] * LOG2E, out)
    return out


def _reduce_rows(x, op):
    slab = 8
    parts = [x[r:r + slab] for r in range(0, x.shape[0], slab)]
    while len(parts) > 1:
        nxt = [op(parts[a], parts[a + 1]) for a in range(0, len(parts) - 1, 2)]
        if len(parts) % 2:
            nxt.append(parts[-1])
        parts = nxt
    red = jnp.max if op is jnp.maximum else jnp.sum
    return red(parts[0], axis=0, keepdims=True)


def _lambda(lq1_ref, lk1_ref, lq2_ref, lk2_ref, lambda_init):
    s1 = jnp.sum(lq1_ref[...] * lk1_ref[...], axis=-1, keepdims=True)
    s2 = jnp.sum(lq2_ref[...] * lk2_ref[...], axis=-1, keepdims=True)
    return jnp.exp(s1) - jnp.exp(s2) + lambda_init


def _prompt_attn_kernel(rb_ref, lq1_ref, lk1_ref, lq2_ref, lk2_ref, q_ref, k_ref, v_ref,
                        bucket_ref, g_ref, o_ref,
                        qq_ref, vt_ref, bias_ref, s_ref, smax_ref, m_ref, l_ref, acc_ref,
                        *, lambda_init):
    bq = q_ref.shape[0]
    n_blocks = vt_ref.shape[0]
    h = pl.program_id(0)
    i = pl.program_id(1)

    @pl.when(i == 0)
    def _():
        bias_ref[0] = _bias_from_buckets(bucket_ref[0], rb_ref, h)
        kc = lax.broadcasted_iota(jnp.int32, (bq, bq), 0) // CHUNK
        qc = lax.broadcasted_iota(jnp.int32, (bq, bq), 1) // CHUNK
        bias_ref[1] = jnp.where(kc <= qc, _bias_from_buckets(bucket_ref[1], rb_ref, h), -jnp.inf)

        def transpose_block(jb, carry):
            rows = pl.ds(pl.multiple_of(jb * bq, bq), bq)
            vt_ref[jb] = v_ref[rows, :].astype(_F32).T.astype(_BF16)
            return carry

        lax.fori_loop(0, n_blocks, transpose_block, 0)

    qt = q_ref[...].astype(_F32).T.astype(_BF16)
    zeros = jnp.zeros((HEAD_DIM, bq), _BF16)
    qq_ref[0:HEAD_DIM, 0:bq] = qt[0:HEAD_DIM]
    qq_ref[HEAD_DIM:, 0:bq] = zeros
    qq_ref[0:HEAD_DIM, bq:] = zeros
    qq_ref[HEAD_DIM:, bq:] = qt[HEAD_DIM:]

    m_ref[...] = jnp.full(m_ref.shape, -jnp.inf, _F32)
    l_ref[...] = jnp.zeros(l_ref.shape, _F32)
    acc_ref[...] = jnp.zeros(acc_ref.shape, _F32)

    tiles = list(range(0, 2 * bq, SCORE_COLS))

    def issue_scores(j, bias_tile):
        k_blk = k_ref[pl.ds(pl.multiple_of(j * bq, bq), bq), :]

        def issue(n):
            c0 = tiles[n]
            s = _dot(k_blk, qq_ref[:, c0:c0 + SCORE_COLS])
            if bias_tile is not None:
                q0 = c0 % bq
                s = s + bias_ref[bias_tile, :, q0:q0 + SCORE_COLS]
            s_ref[n] = s
            smax_ref[n] = _reduce_rows(s, jnp.maximum)

        return issue

    def block(j, shift, issue_next):
        vt = vt_ref[j]
        deferred = None
        for n, c0 in enumerate(tiles):
            cols = slice(c0, c0 + SCORE_COLS)
            m_old = m_ref[:, cols]
            m_new = jnp.maximum(m_old, smax_ref[n] + shift)
            p = jnp.exp2(s_ref[n] - (m_new - shift))
            alpha = jnp.exp2(m_old - m_new)
            l_ref[:, cols] = alpha * l_ref[:, cols] + _reduce_rows(p, jnp.add)
            m_ref[:, cols] = m_new
            pv = _dot(vt, p.astype(_BF16))
            if issue_next is not None:
                issue_next(n)
            if deferred is not None:
                dcols, dalpha, dpv = deferred
                acc_ref[:, dcols] = dalpha * acc_ref[:, dcols] + dpv
            deferred = (cols, alpha, pv)
        dcols, dalpha, dpv = deferred
        acc_ref[:, dcols] = dalpha * acc_ref[:, dcols] + dpv

    far_bias = rb_ref[FAR_BUCKET, h] * LOG2E
    n_far = jnp.maximum(i - 1, 0)
    kind_of_first = jnp.minimum(i, 2)
    for kind, tile in ((0, 1), (1, 0), (2, None)):
        @pl.when(kind_of_first == kind)
        def _(tile=tile):
            issue = issue_scores(0, tile)
            for n in range(len(tiles)):
                issue(n)

    def far_then_far(j):
        block(j, far_bias, issue_scores(j + 1, None))

    def far_group(t, carry):
        for u in range(FAR_UNROLL):
            far_then_far(FAR_UNROLL * t + u)
        return carry

    def far_single(j, carry):
        far_then_far(j)
        return carry

    n_loop = jnp.maximum(n_far - 1, 0)
    n_grouped = n_loop // FAR_UNROLL
    lax.fori_loop(0, n_grouped, far_group, 0)
    lax.fori_loop(n_grouped * FAR_UNROLL, n_loop, far_single, 0)

    @pl.when(n_far >= 1)
    def _():
        block(n_far - 1, far_bias, issue_scores(i - 1, 0))

    @pl.when(i >= 1)
    def _():
        block(i - 1, 0.0, issue_scores(i, 1))

    block(i, 0.0, None)

    lam = _lambda(lq1_ref, lk1_ref, lq2_ref, lk2_ref, lambda_init)
    o = acc_ref[...] / l_ref[...]
    d = o[:, 0:bq] - lam * o[:, bq:]
    y = d * lax.rsqrt(jnp.mean(d * d, axis=0, keepdims=True) + SUBLN_EPS)
    o_ref[...] = (y.T * g_ref[...] * (1.0 - lambda_init)).astype(_BF16)


def _prompt_attention(q, kb, vb, tp, rel_bias, lam_vecs, subln_g, layer, lambda_init):
    d_attn = q.shape[1]
    n_heads = d_attn // HEAD_COLS
    bq = _pick_tile(tp, (512, 256))
    assert bq + 1 >= MAX_DISTANCE and bq % CHUNK == 0 and bq % SCORE_COLS == 0
    nq = tp // bq
    pos = jnp.arange(bq, dtype=jnp.int32)
    bucket = jnp.stack([_t5_bucket(pos[:, None] - (pos[None, :] + bq)),
                        _t5_bucket(pos[:, None] - pos[None, :])])
    lam_spec = _resident((None, 1, HEAD_DIM), lambda h, i: (layer, 0, 0))
    return pl.pallas_call(
        functools.partial(_prompt_attn_kernel, lambda_init=lambda_init),
        grid=(n_heads, nq),
        in_specs=[
            pl.BlockSpec(memory_space=pltpu.SMEM),
            lam_spec, lam_spec, lam_spec, lam_spec,
            pl.BlockSpec((bq, HEAD_COLS), lambda h, i: (i, h)),
            pl.BlockSpec((tp, HEAD_COLS), lambda h, i: (0, h)),
            pl.BlockSpec((tp, HEAD_COLS), lambda h, i: (0, h)),
            _resident((2, bq, bq), lambda h, i: (0, 0, 0)),
            _resident((None, 1, HEAD_COLS), lambda h, i: (layer, 0, 0)),
        ],
        out_specs=pl.BlockSpec((bq, HEAD_COLS), lambda h, i: (i, h)),
        out_shape=jax.ShapeDtypeStruct((tp, d_attn), _BF16),
        scratch_shapes=[
            pltpu.VMEM((HEAD_COLS, 2 * bq), _BF16),
            pltpu.VMEM((nq, HEAD_COLS, bq), _BF16),
            pltpu.VMEM((2, bq, bq), _F32),
            pltpu.VMEM((2 * bq // SCORE_COLS, bq, SCORE_COLS), _F32),
            pltpu.VMEM((2 * bq // SCORE_COLS, 1, SCORE_COLS), _F32),
            pltpu.VMEM((1, 2 * bq), _F32),
            pltpu.VMEM((1, 2 * bq), _F32),
            pltpu.VMEM((HEAD_COLS, 2 * bq), _F32),
        ],
        compiler_params=_params("arbitrary", "arbitrary"),
        name="prompt_attention",
    )(rel_bias, *lam_vecs, q, kb, vb, bucket, subln_g)


def _sample_attn_kernel(rb_ref, lq1_ref, lk1_ref, lq2_ref, lk2_ref, q_ref, kn_ref, vn_ref,
                        kc_ref, vc_ref, bucket_c_ref, bucket_n_ref, g_ref, o_ref,
                        bias_c_ref, bias_n_ref, qq_ref, m_ref, l_ref, acc_ref, *, lambda_init):
    ts = q_ref.shape[0]
    n_heads = qq_ref.shape[0]
    chunk = kc_ref.shape[0] // n_heads
    b = pl.program_id(0)
    c = pl.program_id(1)

    @pl.when((b == 0) & (c == 0))
    def _():
        for h in range(n_heads):
            bias_n_ref[h] = _bias_from_buckets(bucket_n_ref[...], rb_ref, h)
            for cc in range(bias_c_ref.shape[1]):
                bias_c_ref[h, cc] = _bias_from_buckets(bucket_c_ref[cc], rb_ref, h)

    def update(h, s, v):
        m_old = m_ref[h]
        m_new = jnp.maximum(m_old, jnp.max(s, axis=-1, keepdims=True))
        p = jnp.exp2(s - m_new)
        alpha = jnp.exp2(m_old - m_new)
        l_ref[h] = alpha * l_ref[h] + jnp.sum(p, axis=-1, keepdims=True)
        acc_ref[h] = alpha * acc_ref[h] + _dot(p.astype(_BF16), v)
        m_ref[h] = m_new

    @pl.when(c == 0)
    def _():
        m_ref[...] = jnp.full(m_ref.shape, -jnp.inf, _F32)
        l_ref[...] = jnp.zeros(l_ref.shape, _F32)
        acc_ref[...] = jnp.zeros(acc_ref.shape, _F32)
        for h in range(n_heads):
            hc = slice(h * HEAD_COLS, (h + 1) * HEAD_COLS)
            q = q_ref[:, hc]
            lane = lax.broadcasted_iota(jnp.int32, q.shape, 1)
            zero = jnp.zeros_like(q)
            qq_ref[h] = jnp.concatenate([jnp.where(lane < HEAD_DIM, q, zero),
                                         jnp.where(lane >= HEAD_DIM, q, zero)], axis=0)
            bn = bias_n_ref[h]
            s = (_dot_nt(qq_ref[h], kn_ref[:, hc].astype(_BF16))
                 + jnp.concatenate([bn, bn], axis=0))
            update(h, s, vn_ref[:, hc].astype(_BF16))

    for h in range(n_heads):
        head_rows = pl.ds(h, chunk, stride=n_heads)
        bc = bias_c_ref[h, c]
        s = (_dot_nt(qq_ref[h], kc_ref[head_rows, :].astype(_BF16))
             + jnp.concatenate([bc, bc], axis=0))
        update(h, s, vc_ref[head_rows, :].astype(_BF16))

    @pl.when(c == pl.num_programs(1) - 1)
    def _():
        lam = _lambda(lq1_ref, lk1_ref, lq2_ref, lk2_ref, lambda_init)
        for h in range(n_heads):
            o = acc_ref[h] / l_ref[h]
            d = o[:ts] - lam * o[ts:]
            o_ref[:, h * HEAD_COLS:(h + 1) * HEAD_COLS] = (
                _rmsnorm(d, g_ref[...], SUBLN_EPS) * (1.0 - lambda_init)).astype(_BF16)


def _sample_attention(q, k, v, cache_k, cache_v, tp, bs, ts, rel_bias, lam_vecs, subln_g,
                      layer, lambda_init):
    d_attn = q.shape[1]
    n_heads = d_attn // HEAD_COLS
    past = cache_k.shape[2] // n_heads
    assert tp % ts == 0 and ts % 16 == 0
    assert (past + ts - 1) // CHUNK <= past // CHUNK, "new frames must share the queries' chunk"
    chunk = _pick_tile(past, (1024, 512, 256, 128))
    n_chunks = past // chunk
    row0 = tp // ts
    q_pos = past + jnp.arange(ts, dtype=jnp.int32)
    bucket_c = _t5_bucket(jnp.arange(past, dtype=jnp.int32)[None, :] - q_pos[:, None])
    bucket_c = bucket_c.reshape(ts, n_chunks, chunk).transpose(1, 0, 2)
    bucket_n = _t5_bucket(q_pos[None, :] - q_pos[:, None])
    new_rows = pl.BlockSpec((ts, d_attn), lambda b, c: (row0 + b, 0))
    cache = pl.BlockSpec((None, None, chunk * n_heads, HEAD_COLS), lambda b, c: (layer, b, c, 0))
    lam_spec = _resident((None, 1, HEAD_DIM), lambda b, c: (layer, 0, 0))
    return pl.pallas_call(
        functools.partial(_sample_attn_kernel, lambda_init=lambda_init),
        grid=(bs, n_chunks),
        in_specs=[
            pl.BlockSpec(memory_space=pltpu.SMEM),
            lam_spec, lam_spec, lam_spec, lam_spec,
            new_rows, new_rows, new_rows, cache, cache,
            _resident((n_chunks, ts, chunk), lambda b, c: (0, 0, 0)),
            _resident((ts, ts), lambda b, c: (0, 0)),
            _resident((None, 1, HEAD_COLS), lambda b, c: (layer, 0, 0)),
        ],
        out_specs=pl.BlockSpec((ts, d_attn), lambda b, c: (b, 0)),
        out_shape=jax.ShapeDtypeStruct((bs * ts, d_attn), _BF16),
        scratch_shapes=[
            pltpu.VMEM((n_heads, n_chunks, ts, chunk), _F32),
            pltpu.VMEM((n_heads, ts, ts), _F32),
            pltpu.VMEM((n_heads, 2 * ts, HEAD_COLS), _BF16),
            pltpu.VMEM((n_heads, 2 * ts, 1), _F32),
            pltpu.VMEM((n_heads, 2 * ts, 1), _F32),
            pltpu.VMEM((n_heads, 2 * ts, HEAD_COLS), _F32),
        ],
        compiler_params=_params("arbitrary", "arbitrary"),
        name="sample_attention",
    )(rel_bias, *lam_vecs, q, k, v, cache_k, cache_v, bucket_c, bucket_n, subln_g)


def _merge_kernel(x_ref, c_ref, on_ref, g_ref, wga_ref, wgb_ref, wco_ref, wao_ref, wout_ref, y_ref):
    x = x_ref[...]
    h = _rmsnorm(x, g_ref[...], RMS_EPS).astype(_BF16)
    gate_a = jax.nn.sigmoid(_dot(h, wga_ref[...]))
    gate_b = jax.nn.sigmoid(_dot(h, wgb_ref[...]))
    merged = gate_a * _dot(c_ref[...], wco_ref[...]) + gate_b * _dot(on_ref[...], wao_ref[...])
    y_ref[...] = x + _dot(merged.astype(_BF16), wout_ref[...])


def _merge(x, c, o_n, norm_g, w_in_b, w_conv_out_b, w_attn_o_b, w_out_b, layer):
    r, d = x.shape
    tm = _pick_tile(r, (640, 512, 320, 256, 128, 64, 32, 16))
    gate_col0 = w_in_b.shape[2] // d - 2
    row = pl.BlockSpec((tm, d), lambda i: (i, 0))
    sq = _resident((None, d, d), lambda i: (layer, 0, 0))
    return pl.pallas_call(
        _merge_kernel,
        grid=(r // tm,),
        in_specs=[
            row, row, row,
            _resident((None, 1, d), lambda i: (layer, 0, 0)),
            _resident((None, d, d), lambda i: (layer, 0, gate_col0)),
            _resident((None, d, d), lambda i: (layer, 0, gate_col0 + 1)),
            sq, sq, sq,
        ],
        out_specs=row,
        out_shape=jax.ShapeDtypeStruct((r, d), _F32),
        compiler_params=_params("parallel"),
        name="merge_out_proj",
    )(x, c, o_n, norm_g, w_in_b, w_in_b, w_conv_out_b, w_attn_o_b, w_out_b)


def _ffn_kernel(x_ref, g_ref, wfi_ref, wfo_ref, gf_ref, y_ref, *, final_norm):
    d_ff = wfo_ref.shape[0]
    x = x_ref[...]
    h = _rmsnorm(x, g_ref[...], RMS_EPS).astype(_BF16)
    y = x
    half = (d_ff // 2) // 256 * 256
    for c0, c1 in ((0, half), (half, d_ff)):
        gate = _dot(h, wfi_ref[:, c0:c1])
        up = _dot(h, wfi_ref[:, d_ff + c0:d_ff + c1])
        act = (gate * jax.nn.sigmoid(gate) * up).astype(_BF16)
        y = y + _dot(act, wfo_ref[c0:c1, :])
    if final_norm:
        y = _rmsnorm(y, gf_ref[...], RMS_EPS)
    y_ref[...] = y


def _ffn(x, norm_g, w_ffn_in_b, w_ffn_out_b, norm_final, layer, final_norm):
    r, d = x.shape
    d_ff = w_ffn_out_b.shape[1]
    tm = _pick_tile(r, (640, 512, 320, 256, 128, 64, 32, 16))
    row = pl.BlockSpec((tm, d), lambda i: (i, 0))
    return pl.pallas_call(
        functools.partial(_ffn_kernel, final_norm=final_norm),
        grid=(r // tm,),
        in_specs=[
            row,
            _resident((None, 1, d), lambda i: (layer, 0, 0)),
            _resident((None, d, 2 * d_ff), lambda i: (layer, 0, 0)),
            _resident((None, d_ff, d), lambda i: (layer, 0, 0)),
            _resident((1, d), lambda i: (0, 0)),
        ],
        out_specs=row,
        out_shape=jax.ShapeDtypeStruct((r, d), _F32),
        compiler_params=_params("parallel"),
        name="ffn",
    )(x, norm_g, w_ffn_in_b, w_ffn_out_b, norm_final)


def kernel(x_prompt, x_sample, cache_k, cache_v, state_conv, norm_mix, w_in, conv_dw, conv_dw_b,
           conv_ln_g, conv_ln_b, w_conv_out, lambda_q1, lambda_k1, lambda_q2, lambda_k2,
           attn_subln_g, w_attn_o, w_out, norm_ffn, w_ffn_in, w_ffn_out, rel_bias, norm_final):
    depth = w_in.shape[0]
    bp, tp, d = x_prompt.shape
    bs, ts, _ = x_sample.shape
    past = cache_k.shape[2]
    n_heads = cache_k.shape[3]
    assert bp == 1, "one new stream per step"
    assert n_heads == N_HEADS and cache_k.shape[4] == HEAD_COLS
    assert rel_bias.shape == (N_BUCKETS, N_HEADS)
    assert ts >= CONV_STATE and ts % HALO_ROWS == 0 and tp % ts == 0
    dc = conv_dw.shape[2]

    x = jnp.concatenate([x_prompt.reshape(tp, d), x_sample.reshape(bs * ts, d)], axis=0)

    as_rows = lambda a: a.reshape(depth, 1, a.shape[-1])
    norm_mix3, norm_ffn3 = as_rows(norm_mix), as_rows(norm_ffn)
    conv_b3, ln_g3, ln_b3 = as_rows(conv_dw_b), as_rows(conv_ln_g), as_rows(conv_ln_b)
    subln3 = as_rows(attn_subln_g)
    lam3 = [as_rows(a) for a in (lambda_q1, lambda_k1, lambda_q2, lambda_k2)]
    norm_final2 = norm_final.reshape(1, d)
    rel_bias = rel_bias.astype(_F32)
    w_in_b = w_in.astype(_BF16)
    w_conv_out_b = w_conv_out.astype(_BF16)
    w_attn_o_b = w_attn_o.astype(_BF16)
    w_out_b = w_out.astype(_BF16)
    w_ffn_in_b = w_ffn_in.astype(_BF16)
    w_ffn_out_b = w_ffn_out.astype(_BF16)
    cache_k4 = cache_k.reshape(depth, bs, past * n_heads, HEAD_COLS)
    cache_v4 = cache_v.reshape(depth, bs, past * n_heads, HEAD_COLS)
    halo_s = jnp.pad(state_conv, ((0, 0), (0, 0), (HALO_ROWS - CONV_STATE, 0), (0, 0)))
    halo_s = halo_s.reshape(depth, bs * HALO_ROWS, dc)

    tm_conv = _pick_tile(tp, (128, 64, 32))
    halo_per_tile = tm_conv // HALO_ROWS

    k_p, v_p, c_p, k_s, v_s, c_s = [], [], [], [], [], []
    for layer in range(depth):
        lambda_init = 0.8 - 0.6 * math.exp(-0.3 * layer)
        glu, q, k, v, kb, vb = _in_proj(x, norm_mix3, w_in_b, layer)

        conv_args = (conv_dw, conv_b3, ln_g3, ln_b3, layer)
        c_prompt = _conv_branch(
            glu, glu, tm_conv, tp // tm_conv, 0,
            lambda i: (jnp.maximum(i * halo_per_tile - 1, 0), 0), True, *conv_args)
        c_sample = _conv_branch(
            glu, halo_s[layer], ts, bs, tp // ts, lambda i: (i, 0), False, *conv_args)
        c = jnp.concatenate([c_prompt, c_sample], axis=0)

        o_prompt = _prompt_attention(q, kb, vb, tp, rel_bias, lam3, subln3, layer, lambda_init)
        o_sample = _sample_attention(q, k, v, cache_k4, cache_v4, tp, bs, ts, rel_bias, lam3,
                                     subln3, layer, lambda_init)
        o_n = jnp.concatenate([o_prompt, o_sample], axis=0)

        x = _merge(x, c, o_n, norm_mix3, w_in_b, w_conv_out_b, w_attn_o_b, w_out_b, layer)
        x = _ffn(x, norm_ffn3, w_ffn_in_b, w_ffn_out_b, norm_final2, layer, layer == depth - 1)

        k_p.append(k[:tp].reshape(1, tp, n_heads, HEAD_COLS))
        v_p.append(v[:tp].reshape(1, tp, n_heads, HEAD_COLS))
        c_p.append(glu[tp - CONV_STATE:tp].reshape(1, CONV_STATE, dc))
        k_s.append(k[tp:].reshape(bs, ts, n_heads, HEAD_COLS))
        v_s.append(v[tp:].reshape(bs, ts, n_heads, HEAD_COLS))
        c_s.append(glu[tp:].reshape(bs, ts, dc)[:, ts - CONV_STATE:])

    y_prompt = x[:tp].reshape(1, tp, d)
    y_sample = x[tp:].reshape(bs, ts, d)
    return (y_prompt, y_sample, jnp.stack(k_p), jnp.stack(v_p), jnp.stack(c_p),
            jnp.stack(k_s), jnp.stack(v_s), jnp.stack(c_s))
```

```python
import functools
import math

import jax
import jax.numpy as jnp
from jax import lax
from jax.experimental import pallas as pl
from jax.experimental.pallas import tpu as pltpu

N_HEADS = 8
HEAD_DIM = 64
HEAD_COLS = 2 * HEAD_DIM
CONV_WIDTH = 31
CONV_STATE = CONV_WIDTH - 1
CHUNK = 64
N_BUCKETS = 32
MAX_DISTANCE = 128
FAR_BUCKET = N_BUCKETS // 2 - 1
RMS_EPS = 1e-6
LN_EPS = 1e-5
SUBLN_EPS = 1e-5
LOG2E = math.log2(math.e)

HALO_ROWS = 32
SCORE_COLS = 256
FAR_UNROLL = 4
V7X_VMEM_LIMIT_BYTES = 56 * 1024 * 1024

_F32 = jnp.float32
_BF16 = jnp.bfloat16


def _pick_tile(n, candidates):
    for c in candidates:
        if n % c == 0:
            return c
    raise ValueError(f"no tile in {candidates} divides {n}")


def _rmsnorm(x, g, eps):
    return x * lax.rsqrt(jnp.mean(x * x, axis=-1, keepdims=True) + eps) * g


def _dot(a, b):
    return jnp.dot(a, b, preferred_element_type=_F32)


def _dot_nt(a, b):
    return lax.dot_general(a, b, (((1,), (1,)), ((), ())), preferred_element_type=_F32)


def _params(*semantics):
    return pltpu.CompilerParams(dimension_semantics=semantics,
                                vmem_limit_bytes=V7X_VMEM_LIMIT_BYTES)


def _resident(block_shape, index_map):
    return pl.BlockSpec(block_shape, index_map, pipeline_mode=pl.Buffered(1))


def _in_proj_kernel(x_ref, g_ref, w_ref, *refs):
    glu_ref, q_ref, k_ref, v_ref, kb_ref, vb_ref = refs[-6:]
    d = x_ref.shape[1]
    h = _rmsnorm(x_ref[...], g_ref[...], RMS_EPS).astype(_BF16)

    def proj(c):
        return _dot(h, w_ref[:, c * d:(c + 1) * d])

    glu_ref[...] = proj(0) * jax.nn.sigmoid(proj(1))
    q_ref[...] = (proj(2) * (HEAD_DIM ** -0.5 * LOG2E)).astype(_BF16)
    tm = x_ref.shape[0]
    for full_ref, bf16_ref, col in ((k_ref, kb_ref, 3), (v_ref, vb_ref, 4)):
        y = proj(col)
        bf16_ref[...] = y.astype(_BF16)
        for head in range(N_HEADS):
            full_ref[pl.ds(head, tm, stride=N_HEADS), :] = (
                y[:, head * HEAD_COLS:(head + 1) * HEAD_COLS])


def _in_proj(x, norm_g, w_in_b, layer, kv_stacked=None):
    r, d = x.shape
    assert d == N_HEADS * HEAD_COLS
    tm = _pick_tile(r, (512, 256, 128, 64, 32, 16))
    n_tiles = r // tm
    row = pl.BlockSpec((tm, d), lambda i: (i, 0))
    f32_out = jax.ShapeDtypeStruct((r, d), _F32)
    bf16_out = jax.ShapeDtypeStruct((r, d), _BF16)
    in_specs = [
        row,
        _resident((None, 1, d), lambda i: (layer, 0, 0)),
        _resident((None, d, 5 * d), lambda i: (layer, 0, 0)),
    ]
    if kv_stacked is None:
        kv_stacked = ()
        aliases = {}
        tile0 = 0
        head_rows_out = jax.ShapeDtypeStruct((r * N_HEADS, HEAD_COLS), _F32)
    else:
        in_specs += [pl.BlockSpec(memory_space=pl.ANY)] * 2
        aliases = {3: 2, 4: 3}
        tile0 = layer * n_tiles
        head_rows_out = jax.ShapeDtypeStruct(kv_stacked[0].shape, _F32)
    head_rows = pl.BlockSpec((tm * N_HEADS, HEAD_COLS), lambda i: (tile0 + i, 0))
    return pl.pallas_call(
        _in_proj_kernel,
        grid=(n_tiles,),
        in_specs=in_specs,
        out_specs=[row, row, head_rows, head_rows, row, row],
        out_shape=[f32_out, bf16_out, head_rows_out, head_rows_out, bf16_out, bf16_out],
        input_output_aliases=aliases,
        compiler_params=_params("parallel"),
        name="in_proj",
    )(x, norm_g, w_in_b, *kv_stacked)


def _conv_kernel(halo_ref, glu_ref, w_ref, b_ref, lng_ref, lnb_ref, c_ref, pad_ref, dw_ref,
                 *, zero_first_halo):
    tm, dc = glu_ref.shape
    halo = halo_ref[...]
    if zero_first_halo:
        halo = jnp.where(pl.program_id(0) == 0, jnp.zeros_like(halo), halo)
    pad_ref[0:HALO_ROWS, :] = halo
    pad_ref[HALO_ROWS:, :] = glu_ref[...]

    first = HALO_ROWS - CONV_STATE
    rows = _pick_tile(tm, (64, 32))
    lanes = 128
    for r0 in range(0, tm, rows):
        for c0 in range(0, dc, lanes):
            cs = slice(c0, c0 + lanes)
            acc = jnp.broadcast_to(b_ref[:, cs], (rows, lanes))
            for b in range(8):
                ext = rows + (8 if b else 0)
                part = None
                for a in range((first + CONV_WIDTH - 1) // 8 + 1):
                    j = 8 * a + b - first
                    if 0 <= j < CONV_WIDTH:
                        term = w_ref[j:j + 1, cs] * pad_ref[r0 + 8 * a:r0 + 8 * a + ext, cs]
                        part = term if part is None else part + term
                if b:
                    part = pltpu.roll(part, ext - b, axis=0)[:rows]
                acc = acc + part
            dw_ref[r0:r0 + rows, cs] = acc

    dw = dw_ref[...]
    mu = jnp.mean(dw, axis=-1, keepdims=True)
    cen = dw - mu
    var = jnp.mean(cen * cen, axis=-1, keepdims=True)
    y = cen * lax.rsqrt(var + LN_EPS) * lng_ref[...] + lnb_ref[...]
    c_ref[...] = (y * jax.nn.sigmoid(y)).astype(_BF16)


def _conv_branch(glu, halo_src, tm, n_tiles, halo_map, zero_first_halo,
                 conv_dw, conv_dw_b, conv_ln_g, conv_ln_b, layer):
    dc = glu.shape[1]
    vec = _resident((None, 1, dc), lambda i: (layer, 0, 0))
    return pl.pallas_call(
        functools.partial(_conv_kernel, zero_first_halo=zero_first_halo),
        grid=(n_tiles,),
        in_specs=[
            pl.BlockSpec((HALO_ROWS, dc), halo_map),
            pl.BlockSpec((tm, dc), lambda i: (i, 0)),
            _resident((None, CONV_WIDTH, dc), lambda i: (layer, 0, 0)),
            vec, vec, vec,
        ],
        out_specs=pl.BlockSpec((tm, dc), lambda i: (i, 0)),
        out_shape=jax.ShapeDtypeStruct((n_tiles * tm, dc), _BF16),
        scratch_shapes=[pltpu.VMEM((tm + HALO_ROWS, dc), _F32), pltpu.VMEM((tm, dc), _F32)],
        compiler_params=_params("parallel"),
        name="conv_branch",
    )(halo_src, glu, conv_dw, conv_dw_b, conv_ln_g, conv_ln_b)


def _t5_bucket(rel):
    half = N_BUCKETS // 2
    max_exact = half // 2
    ret = jnp.where(rel > 0, half, 0)
    n = jnp.abs(rel)
    nf = jnp.maximum(n, 1).astype(jnp.float32)
    large = max_exact + (jnp.log(nf / max_exact) / math.log(MAX_DISTANCE / max_exact)
                         * (half - max_exact)).astype(jnp.int32)
    large = jnp.minimum(large, half - 1)
    return ret + jnp.where(n < max_exact, n, large)


def _bias_from_buckets(bucket, rb_ref, h):
    out = jnp.zeros(bucket.shape, _F32)
    for b in range(N_BUCKETS):
        out = jnp.where(bucket == b, rb_ref[b, h] * LOG2E, out)
    return out


def _reduce_rows(x, op):
    slab = 8
    parts = [x[r:r + slab] for r in range(0, x.shape[0], slab)]
    while len(parts) > 1:
        nxt = [op(parts[a], parts[a + 1]) for a in range(0, len(parts) - 1, 2)]
        if len(parts) % 2:
            nxt.append(parts[-1])
        parts = nxt
    red = jnp.max if op is jnp.maximum else jnp.sum
    return red(parts[0], axis=0, keepdims=True)


def _lambda(lq1_ref, lk1_ref, lq2_ref, lk2_ref, lambda_init):
    s1 = jnp.sum(lq1_ref[...] * lk1_ref[...], axis=-1, keepdims=True)
    s2 = jnp.sum(lq2_ref[...] * lk2_ref[...], axis=-1, keepdims=True)
    return jnp.exp(s1) - jnp.exp(s2) + lambda_init


def _prompt_attn_kernel(rb_ref, lq1_ref, lk1_ref, lq2_ref, lk2_ref, q_ref, k_ref, v_ref,
                        bucket_ref, g_ref, o_ref,
                        qq_ref, vt_ref, bias_ref, s_ref, smax_ref, m_ref, l_ref, acc_ref,
                        *, lambda_init):
    bq = q_ref.shape[0]
    n_blocks = vt_ref.shape[0]
    h = pl.program_id(0)
    i = pl.program_id(1)

    @pl.when(i == 0)
    def _():
        bias_ref[0] = _bias_from_buckets(bucket_ref[0], rb_ref, h)
        kc = lax.broadcasted_iota(jnp.int32, (bq, bq), 0) // CHUNK
        qc = lax.broadcasted_iota(jnp.int32, (bq, bq), 1) // CHUNK
        bias_ref[1] = jnp.where(kc <= qc, _bias_from_buckets(bucket_ref[1], rb_ref, h), -jnp.inf)

        def transpose_block(jb, carry):
            rows = pl.ds(pl.multiple_of(jb * bq, bq), bq)
            vt_ref[jb] = v_ref[rows, :].astype(_F32).T.astype(_BF16)
            return carry

        lax.fori_loop(0, n_blocks, transpose_block, 0)

    qt = q_ref[...].astype(_F32).T.astype(_BF16)
    zeros = jnp.zeros((HEAD_DIM, bq), _BF16)
    qq_ref[0:HEAD_DIM, 0:bq] = qt[0:HEAD_DIM]
    qq_ref[HEAD_DIM:, 0:bq] = zeros
    qq_ref[0:HEAD_DIM, bq:] = zeros
    qq_ref[HEAD_DIM:, bq:] = qt[HEAD_DIM:]

    m_ref[...] = jnp.full(m_ref.shape, -jnp.inf, _F32)
    l_ref[...] = jnp.zeros(l_ref.shape, _F32)
    acc_ref[...] = jnp.zeros(acc_ref.shape, _F32)

    tiles = list(range(0, 2 * bq, SCORE_COLS))

    def issue_scores(j, bias_tile):
        k_blk = k_ref[pl.ds(pl.multiple_of(j * bq, bq), bq), :]

        def issue(n):
            c0 = tiles[n]
            s = _dot(k_blk, qq_ref[:, c0:c0 + SCORE_COLS])
            if bias_tile is not None:
                q0 = c0 % bq
                s = s + bias_ref[bias_tile, :, q0:q0 + SCORE_COLS]
            s_ref[n] = s
            smax_ref[n] = _reduce_rows(s, jnp.maximum)

        return issue

    def block(j, shift, issue_next):
        vt = vt_ref[j]
        deferred = None
        for n, c0 in enumerate(tiles):
            cols = slice(c0, c0 + SCORE_COLS)
            m_old = m_ref[:, cols]
            m_new = jnp.maximum(m_old, smax_ref[n] + shift)
            p = jnp.exp2(s_ref[n] - (m_new - shift))
            alpha = jnp.exp2(m_old - m_new)
            l_ref[:, cols] = alpha * l_ref[:, cols] + _reduce_rows(p, jnp.add)
            m_ref[:, cols] = m_new
            pv = _dot(vt, p.astype(_BF16))
            if issue_next is not None:
                issue_next(n)
            if deferred is not None:
                dcols, dalpha, dpv = deferred
                acc_ref[:, dcols] = dalpha * acc_ref[:, dcols] + dpv
            deferred = (cols, alpha, pv)
        dcols, dalpha, dpv = deferred
        acc_ref[:, dcols] = dalpha * acc_ref[:, dcols] + dpv

    far_bias = rb_ref[FAR_BUCKET, h] * LOG2E
    n_far = jnp.maximum(i - 1, 0)
    kind_of_first = jnp.minimum(i, 2)
    for kind, tile in ((0, 1), (1, 0), (2, None)):
        @pl.when(kind_of_first == kind)
        def _(tile=tile):
            issue = issue_scores(0, tile)
            for n in range(len(tiles)):
                issue(n)

    def far_then_far(j):
        block(j, far_bias, issue_scores(j + 1, None))

    def far_group(t, carry):
        for u in range(FAR_UNROLL):
            far_then_far(FAR_UNROLL * t + u)
        return carry

    def far_single(j, carry):
        far_then_far(j)
        return carry

    n_loop = jnp.maximum(n_far - 1, 0)
    n_grouped = n_loop // FAR_UNROLL
    lax.fori_loop(0, n_grouped, far_group, 0)
    lax.fori_loop(n_grouped * FAR_UNROLL, n_loop, far_single, 0)

    def finish():
        lam = _lambda(lq1_ref, lk1_ref, lq2_ref, lk2_ref, lambda_init)
        o = acc_ref[...] / l_ref[...]
        d = o[:, 0:bq] - lam * o[:, bq:]
        y = d * lax.rsqrt(jnp.mean(d * d, axis=0, keepdims=True) + SUBLN_EPS)
        o_ref[...] = (y.T * g_ref[...] * (1.0 - lambda_init)).astype(_BF16)

    @pl.when(i >= 2)
    def _():
        block(i - 2, far_bias, issue_scores(i - 1, 0))
        block(i - 1, 0.0, issue_scores(i, 1))
        block(i, 0.0, None)
        finish()

    @pl.when(i == 1)
    def _():
        block(0, 0.0, issue_scores(1, 1))
        block(1, 0.0, None)
        finish()

    @pl.when(i == 0)
    def _():
        block(0, 0.0, None)
        finish()


def _prompt_attention(q, kb, vb, rel_bias, lam_vecs, subln_g, layer, lambda_init):
    tp, d_attn = q.shape
    n_heads = d_attn // HEAD_COLS
    bq = _pick_tile(tp, (512, 256))
    assert bq + 1 >= MAX_DISTANCE and bq % CHUNK == 0 and bq % SCORE_COLS == 0
    nq = tp // bq
    pos = jnp.arange(bq, dtype=jnp.int32)
    bucket = jnp.stack([_t5_bucket(pos[:, None] - (pos[None, :] + bq)),
                        _t5_bucket(pos[:, None] - pos[None, :])])
    lam_spec = _resident((None, 1, HEAD_DIM), lambda h, i: (layer, 0, 0))
    return pl.pallas_call(
        functools.partial(_prompt_attn_kernel, lambda_init=lambda_init),
        grid=(n_heads, nq),
        in_specs=[
            pl.BlockSpec(memory_space=pltpu.SMEM),
            lam_spec, lam_spec, lam_spec, lam_spec,
            pl.BlockSpec((bq, HEAD_COLS), lambda h, i: (i, h)),
            pl.BlockSpec((tp, HEAD_COLS), lambda h, i: (0, h)),
            pl.BlockSpec((tp, HEAD_COLS), lambda h, i: (0, h)),
            _resident((2, bq, bq), lambda h, i: (0, 0, 0)),
            _resident((None, 1, HEAD_COLS), lambda h, i: (layer, 0, 0)),
        ],
        out_specs=pl.BlockSpec((bq, HEAD_COLS), lambda h, i: (i, h)),
        out_shape=jax.ShapeDtypeStruct((tp, d_attn), _BF16),
        scratch_shapes=[
            pltpu.VMEM((HEAD_COLS, 2 * bq), _BF16),
            pltpu.VMEM((nq, HEAD_COLS, bq), _BF16),
            pltpu.VMEM((2, bq, bq), _F32),
            pltpu.VMEM((2 * bq // SCORE_COLS, bq, SCORE_COLS), _F32),
            pltpu.VMEM((2 * bq // SCORE_COLS, 1, SCORE_COLS), _F32),
            pltpu.VMEM((1, 2 * bq), _F32),
            pltpu.VMEM((1, 2 * bq), _F32),
            pltpu.VMEM((HEAD_COLS, 2 * bq), _F32),
        ],
        compiler_params=_params("arbitrary", "arbitrary"),
        name="prompt_attention",
    )(rel_bias, *lam_vecs, q, kb, vb, bucket, subln_g)


def _sample_attn_kernel(rb_ref, lq1_ref, lk1_ref, lq2_ref, lk2_ref, q_ref, kn_ref, vn_ref,
                        kc_ref, vc_ref, bucket_c_ref, bucket_n_ref, g_ref, o_ref,
                        bias_c_ref, bias_n_ref, qq_ref, m_ref, l_ref, acc_ref, *, lambda_init):
    ts = q_ref.shape[0]
    n_heads = qq_ref.shape[0]
    chunk = kc_ref.shape[0] // n_heads
    b = pl.program_id(0)
    c = pl.program_id(1)

    @pl.when((b == 0) & (c == 0))
    def _():
        for h in range(n_heads):
            bias_n_ref[h] = _bias_from_buckets(bucket_n_ref[...], rb_ref, h)
            for cc in range(bias_c_ref.shape[1]):
                bias_c_ref[h, cc] = _bias_from_buckets(bucket_c_ref[cc], rb_ref, h)

    def update(h, s, v):
        m_old = m_ref[h]
        m_new = jnp.maximum(m_old, jnp.max(s, axis=-1, keepdims=True))
        p = jnp.exp2(s - m_new)
        alpha = jnp.exp2(m_old - m_new)
        l_ref[h] = alpha * l_ref[h] + jnp.sum(p, axis=-1, keepdims=True)
        acc_ref[h] = alpha * acc_ref[h] + _dot(p.astype(_BF16), v)
        m_ref[h] = m_new

    @pl.when(c == 0)
    def _():
        m_ref[...] = jnp.full(m_ref.shape, -jnp.inf, _F32)
        l_ref[...] = jnp.zeros(l_ref.shape, _F32)
        acc_ref[...] = jnp.zeros(acc_ref.shape, _F32)
        for h in range(n_heads):
            hc = slice(h * HEAD_COLS, (h + 1) * HEAD_COLS)
            q = q_ref[:, hc]
            lane = lax.broadcasted_iota(jnp.int32, q.shape, 1)
            zero = jnp.zeros_like(q)
            qq_ref[h] = jnp.concatenate([jnp.where(lane < HEAD_DIM, q, zero),
                                         jnp.where(lane >= HEAD_DIM, q, zero)], axis=0)
            bn = bias_n_ref[h]
            head_rows = pl.ds(h, ts, stride=n_heads)
            s = (_dot_nt(qq_ref[h], kn_ref[head_rows, :].astype(_BF16))
                 + jnp.concatenate([bn, bn], axis=0))
            update(h, s, vn_ref[head_rows, :].astype(_BF16))

    for h in range(n_heads):
        head_rows = pl.ds(h, chunk, stride=n_heads)
        bc = bias_c_ref[h, c]
        s = (_dot_nt(qq_ref[h], kc_ref[head_rows, :].astype(_BF16))
             + jnp.concatenate([bc, bc], axis=0))
        update(h, s, vc_ref[head_rows, :].astype(_BF16))

    @pl.when(c == pl.num_programs(1) - 1)
    def _():
        lam = _lambda(lq1_ref, lk1_ref, lq2_ref, lk2_ref, lambda_init)
        for h in range(n_heads):
            o = acc_ref[h] / l_ref[h]
            d = o[:ts] - lam * o[ts:]
            o_ref[:, h * HEAD_COLS:(h + 1) * HEAD_COLS] = (
                _rmsnorm(d, g_ref[...], SUBLN_EPS) * (1.0 - lambda_init)).astype(_BF16)


def _sample_attention(q, k, v, cache_k, cache_v, bs, ts, rel_bias, lam_vecs, subln_g,
                      layer, lambda_init):
    d_attn = q.shape[1]
    n_heads = d_attn // HEAD_COLS
    past = cache_k.shape[2] // n_heads
    assert ts % 16 == 0
    assert (past + ts - 1) // CHUNK <= past // CHUNK, "new frames must share the queries' chunk"
    chunk = _pick_tile(past, (1024, 512, 256, 128))
    n_chunks = past // chunk
    q_pos = past + jnp.arange(ts, dtype=jnp.int32)
    bucket_c = _t5_bucket(jnp.arange(past, dtype=jnp.int32)[None, :] - q_pos[:, None])
    bucket_c = bucket_c.reshape(ts, n_chunks, chunk).transpose(1, 0, 2)
    bucket_n = _t5_bucket(q_pos[None, :] - q_pos[:, None])
    new_q = pl.BlockSpec((ts, d_attn), lambda b, c: (b, 0))
    new_kv = pl.BlockSpec((ts * n_heads, HEAD_COLS), lambda b, c: (b, 0))
    cache = pl.BlockSpec((None, None, chunk * n_heads, HEAD_COLS), lambda b, c: (layer, b, c, 0))
    lam_spec = _resident((None, 1, HEAD_DIM), lambda b, c: (layer, 0, 0))
    return pl.pallas_call(
        functools.partial(_sample_attn_kernel, lambda_init=lambda_init),
        grid=(bs, n_chunks),
        in_specs=[
            pl.BlockSpec(memory_space=pltpu.SMEM),
            lam_spec, lam_spec, lam_spec, lam_spec,
            new_q, new_kv, new_kv, cache, cache,
            _resident((n_chunks, ts, chunk), lambda b, c: (0, 0, 0)),
            _resident((ts, ts), lambda b, c: (0, 0)),
            _resident((None, 1, HEAD_COLS), lambda b, c: (layer, 0, 0)),
        ],
        out_specs=pl.BlockSpec((ts, d_attn), lambda b, c: (b, 0)),
        out_shape=jax.ShapeDtypeStruct((bs * ts, d_attn), _BF16),
        scratch_shapes=[
            pltpu.VMEM((n_heads, n_chunks, ts, chunk), _F32),
            pltpu.VMEM((n_heads, ts, ts), _F32),
            pltpu.VMEM((n_heads, 2 * ts, HEAD_COLS), _BF16),
            pltpu.VMEM((n_heads, 2 * ts, 1), _F32),
            pltpu.VMEM((n_heads, 2 * ts, 1), _F32),
            pltpu.VMEM((n_heads, 2 * ts, HEAD_COLS), _F32),
        ],
        compiler_params=_params("arbitrary", "arbitrary"),
        name="sample_attention",
    )(rel_bias, *lam_vecs, q, k, v, cache_k, cache_v, bucket_c, bucket_n, subln_g)


def _merge_kernel(x_ref, c_ref, on_ref, g_ref, wga_ref, wgb_ref, wco_ref, wao_ref, wout_ref, y_ref):
    x = x_ref[...]
    h = _rmsnorm(x, g_ref[...], RMS_EPS).astype(_BF16)
    gate_a = jax.nn.sigmoid(_dot(h, wga_ref[...]))
    gate_b = jax.nn.sigmoid(_dot(h, wgb_ref[...]))
    merged = gate_a * _dot(c_ref[...], wco_ref[...]) + gate_b * _dot(on_ref[...], wao_ref[...])
    y_ref[...] = x + _dot(merged.astype(_BF16), wout_ref[...])


def _merge(x, c, o_n, norm_g, w_in_b, w_conv_out_b, w_attn_o_b, w_out_b, layer):
    r, d = x.shape
    tm = _pick_tile(r, (640, 512, 320, 256, 128, 64, 32, 16))
    gate_col0 = w_in_b.shape[2] // d - 2
    row = pl.BlockSpec((tm, d), lambda i: (i, 0))
    sq = _resident((None, d, d), lambda i: (layer, 0, 0))
    return pl.pallas_call(
        _merge_kernel,
        grid=(r // tm,),
        in_specs=[
            row, row, row,
            _resident((None, 1, d), lambda i: (layer, 0, 0)),
            _resident((None, d, d), lambda i: (layer, 0, gate_col0)),
            _resident((None, d, d), lambda i: (layer, 0, gate_col0 + 1)),
            sq, sq, sq,
        ],
        out_specs=row,
        out_shape=jax.ShapeDtypeStruct((r, d), _F32),
        compiler_params=_params("parallel"),
        name="merge_out_proj",
    )(x, c, o_n, norm_g, w_in_b, w_in_b, w_conv_out_b, w_attn_o_b, w_out_b)


def _ffn_kernel(x_ref, g_ref, wfi_ref, wfo_ref, gf_ref, y_ref, *, final_norm):
    d_ff = wfo_ref.shape[0]
    x = x_ref[...]
    h = _rmsnorm(x, g_ref[...], RMS_EPS).astype(_BF16)
    y = x
    half = (d_ff // 2) // 256 * 256
    for c0, c1 in ((0, half), (half, d_ff)):
        gate = _dot(h, wfi_ref[:, c0:c1])
        up = _dot(h, wfi_ref[:, d_ff + c0:d_ff + c1])
        act = (gate * jax.nn.sigmoid(gate) * up).astype(_BF16)
        y = y + _dot(act, wfo_ref[c0:c1, :])
    if final_norm:
        y = _rmsnorm(y, gf_ref[...], RMS_EPS)
    y_ref[...] = y


def _ffn(x, norm_g, w_ffn_in_b, w_ffn_out_b, norm_final, layer, final_norm):
    r, d = x.shape
    d_ff = w_ffn_out_b.shape[1]
    tm = _pick_tile(r, (640, 512, 320, 256, 128, 64, 32, 16))
    row = pl.BlockSpec((tm, d), lambda i: (i, 0))
    return pl.pallas_call(
        functools.partial(_ffn_kernel, final_norm=final_norm),
        grid=(r // tm,),
        in_specs=[
            row,
            _resident((None, 1, d), lambda i: (layer, 0, 0)),
            _resident((None, d, 2 * d_ff), lambda i: (layer, 0, 0)),
            _resident((None, d_ff, d), lambda i: (layer, 0, 0)),
            _resident((1, d), lambda i: (0, 0)),
        ],
        out_specs=row,
        out_shape=jax.ShapeDtypeStruct((r, d), _F32),
        compiler_params=_params("parallel"),
        name="ffn",
    )(x, norm_g, w_ffn_in_b, w_ffn_out_b, norm_final)


def kernel(x_prompt, x_sample, cache_k, cache_v, state_conv, norm_mix, w_in, conv_dw, conv_dw_b,
           conv_ln_g, conv_ln_b, w_conv_out, lambda_q1, lambda_k1, lambda_q2, lambda_k2,
           attn_subln_g, w_attn_o, w_out, norm_ffn, w_ffn_in, w_ffn_out, rel_bias, norm_final):
    depth = w_in.shape[0]
    bp, tp, d = x_prompt.shape
    bs, ts, _ = x_sample.shape
    past = cache_k.shape[2]
    n_heads = cache_k.shape[3]
    assert bp == 1, "one new stream per step"
    assert n_heads == N_HEADS and cache_k.shape[4] == HEAD_COLS
    assert rel_bias.shape == (N_BUCKETS, N_HEADS)
    assert ts >= CONV_STATE and ts % HALO_ROWS == 0 and tp % ts == 0
    dc = conv_dw.shape[2]

    xp = x_prompt.reshape(tp, d)
    xs = x_sample.reshape(bs * ts, d)

    as_rows = lambda a: a.reshape(depth, 1, a.shape[-1])
    norm_mix3, norm_ffn3 = as_rows(norm_mix), as_rows(norm_ffn)
    conv_b3, ln_g3, ln_b3 = as_rows(conv_dw_b), as_rows(conv_ln_g), as_rows(conv_ln_b)
    subln3 = as_rows(attn_subln_g)
    lam3 = [as_rows(a) for a in (lambda_q1, lambda_k1, lambda_q2, lambda_k2)]
    norm_final2 = norm_final.reshape(1, d)
    rel_bias = rel_bias.astype(_F32)
    w_in_b = w_in.astype(_BF16)
    w_conv_out_b = w_conv_out.astype(_BF16)
    w_attn_o_b = w_attn_o.astype(_BF16)
    w_out_b = w_out.astype(_BF16)
    w_ffn_in_b = w_ffn_in.astype(_BF16)
    w_ffn_out_b = w_ffn_out.astype(_BF16)
    cache_k4 = cache_k.reshape(depth, bs, past * n_heads, HEAD_COLS)
    cache_v4 = cache_v.reshape(depth, bs, past * n_heads, HEAD_COLS)
    halo_s = jnp.pad(state_conv, ((0, 0), (0, 0), (HALO_ROWS - CONV_STATE, 0), (0, 0)))
    halo_s = halo_s.reshape(depth, bs * HALO_ROWS, dc)

    tm_conv = _pick_tile(tp, (128, 64, 32))
    halo_per_tile = tm_conv // HALO_ROWS

    k_all = jnp.zeros((depth * tp * n_heads, HEAD_COLS), _F32)
    v_all = jnp.zeros((depth * tp * n_heads, HEAD_COLS), _F32)
    c_p, k_s, v_s, c_s = [], [], [], []
    for layer in range(depth):
        lambda_init = 0.8 - 0.6 * math.exp(-0.3 * layer)
        last = layer == depth - 1
        conv_args = (conv_dw, conv_b3, ln_g3, ln_b3, layer)
        merge_args = (norm_mix3, w_in_b, w_conv_out_b, w_attn_o_b, w_out_b, layer)
        ffn_args = (norm_ffn3, w_ffn_in_b, w_ffn_out_b, norm_final2, layer, last)

        glu, q, k_all, v_all, kb, vb = _in_proj(xp, norm_mix3, w_in_b, layer, (k_all, v_all))
        c = _conv_branch(glu, glu, tm_conv, tp // tm_conv,
                         lambda i: (jnp.maximum(i * halo_per_tile - 1, 0), 0), True, *conv_args)
        o_n = _prompt_attention(q, kb, vb, rel_bias, lam3, subln3, layer, lambda_init)
        xp = _ffn(_merge(xp, c, o_n, *merge_args), *ffn_args)
        c_p.append(glu[tp - CONV_STATE:].reshape(1, CONV_STATE, dc))

        glu, q, k, v, _, _ = _in_proj(xs, norm_mix3, w_in_b, layer)
        c = _conv_branch(glu, halo_s[layer], ts, bs, lambda i: (i, 0), False, *conv_args)
        o_n = _sample_attention(q, k, v, cache_k4, cache_v4, bs, ts, rel_bias, lam3, subln3,
                                layer, lambda_init)
        xs = _ffn(_merge(xs, c, o_n, *merge_args), *ffn_args)
        k_s.append(k.reshape(bs, ts, n_heads, HEAD_COLS))
        v_s.append(v.reshape(bs, ts, n_heads, HEAD_COLS))
        c_s.append(glu.reshape(bs, ts, dc)[:, ts - CONV_STATE:])

    kv_shape = (depth, 1, tp, n_heads, HEAD_COLS)
    return (xp.reshape(1, tp, d), xs.reshape(bs, ts, d), k_all.reshape(kv_shape),
            v_all.reshape(kv_shape), jnp.stack(c_p), jnp.stack(k_s), jnp.stack(v_s),
            jnp.stack(c_s))
```

```python
import functools
import math

import jax
import jax.numpy as jnp
from jax import lax
from jax.experimental import pallas as pl
from jax.experimental.pallas import tpu as pltpu

N_HEADS = 8
HEAD_DIM = 64
HEAD_COLS = 2 * HEAD_DIM
CONV_WIDTH = 31
CONV_STATE = CONV_WIDTH - 1
CHUNK = 64
N_BUCKETS = 32
MAX_DISTANCE = 128
FAR_BUCKET = N_BUCKETS // 2 - 1
RMS_EPS = 1e-6
LN_EPS = 1e-5
SUBLN_EPS = 1e-5
LOG2E = math.log2(math.e)

HALO_ROWS = 32
SCORE_COLS = 256
FAR_GROUPS = (8, 4, 2, 1)
V7X_VMEM_LIMIT_BYTES = 56 * 1024 * 1024

_F32 = jnp.float32
_BF16 = jnp.bfloat16


def _pick_tile(n, candidates):
    for c in candidates:
        if n % c == 0:
            return c
    raise ValueError(f"no tile in {candidates} divides {n}")


def _rmsnorm(x, g, eps):
    return x * lax.rsqrt(jnp.mean(x * x, axis=-1, keepdims=True) + eps) * g


def _dot(a, b):
    return jnp.dot(a, b, preferred_element_type=_F32)


def _dot_nt(a, b):
    return lax.dot_general(a, b, (((1,), (1,)), ((), ())), preferred_element_type=_F32)


def _params(*semantics):
    return pltpu.CompilerParams(dimension_semantics=semantics,
                                vmem_limit_bytes=V7X_VMEM_LIMIT_BYTES)


def _resident(block_shape, index_map):
    return pl.BlockSpec(block_shape, index_map, pipeline_mode=pl.Buffered(1))


def _in_proj_kernel(x_ref, g_ref, w_ref, *refs):
    glu_ref, q_ref, k_ref, v_ref, kb_ref, vb_ref = refs[-6:]
    d = x_ref.shape[1]
    h = _rmsnorm(x_ref[...], g_ref[...], RMS_EPS).astype(_BF16)

    def proj(c):
        return _dot(h, w_ref[:, c * d:(c + 1) * d])

    glu_ref[...] = proj(0) * jax.nn.sigmoid(proj(1))
    q_ref[...] = (proj(2) * (HEAD_DIM ** -0.5 * LOG2E)).astype(_BF16)
    tm = x_ref.shape[0]
    for full_ref, bf16_ref, col in ((k_ref, kb_ref, 3), (v_ref, vb_ref, 4)):
        y = proj(col)
        bf16_ref[...] = y.astype(_BF16)
        for head in range(N_HEADS):
            full_ref[pl.ds(head, tm, stride=N_HEADS), :] = (
                y[:, head * HEAD_COLS:(head + 1) * HEAD_COLS])


def _in_proj(x, norm_g, w_in_b, layer, kv_stacked=None):
    r, d = x.shape
    assert d == N_HEADS * HEAD_COLS
    tm = _pick_tile(r, (512, 256, 128, 64, 32, 16))
    n_tiles = r // tm
    row = pl.BlockSpec((tm, d), lambda i: (i, 0))
    f32_out = jax.ShapeDtypeStruct((r, d), _F32)
    bf16_out = jax.ShapeDtypeStruct((r, d), _BF16)
    in_specs = [
        row,
        _resident((None, 1, d), lambda i: (layer, 0, 0)),
        _resident((None, d, 5 * d), lambda i: (layer, 0, 0)),
    ]
    if kv_stacked is None:
        kv_stacked = ()
        aliases = {}
        tile0 = 0
        head_rows_out = jax.ShapeDtypeStruct((r * N_HEADS, HEAD_COLS), _F32)
    else:
        in_specs += [pl.BlockSpec(memory_space=pl.ANY)] * 2
        aliases = {3: 2, 4: 3}
        tile0 = layer * n_tiles
        head_rows_out = jax.ShapeDtypeStruct(kv_stacked[0].shape, _F32)
    head_rows = pl.BlockSpec((tm * N_HEADS, HEAD_COLS), lambda i: (tile0 + i, 0))
    return pl.pallas_call(
        _in_proj_kernel,
        grid=(n_tiles,),
        in_specs=in_specs,
        out_specs=[row, row, head_rows, head_rows, row, row],
        out_shape=[f32_out, bf16_out, head_rows_out, head_rows_out, bf16_out, bf16_out],
        input_output_aliases=aliases,
        compiler_params=_params("parallel"),
        name="in_proj",
    )(x, norm_g, w_in_b, *kv_stacked)


def _conv_kernel(halo_ref, glu_ref, w_ref, b_ref, lng_ref, lnb_ref, c_ref, pad_ref, dw_ref,
                 *, zero_first_halo):
    tm, dc = glu_ref.shape
    halo = halo_ref[...]
    if zero_first_halo:
        halo = jnp.where(pl.program_id(0) == 0, jnp.zeros_like(halo), halo)
    pad_ref[0:HALO_ROWS, :] = halo
    pad_ref[HALO_ROWS:, :] = glu_ref[...]

    first = HALO_ROWS - CONV_STATE
    rows = _pick_tile(tm, (64, 32))
    lanes = 128
    for r0 in range(0, tm, rows):
        for c0 in range(0, dc, lanes):
            cs = slice(c0, c0 + lanes)
            acc = jnp.broadcast_to(b_ref[:, cs], (rows, lanes))
            for b in range(8):
                ext = rows + (8 if b else 0)
                part = None
                for a in range((first + CONV_WIDTH - 1) // 8 + 1):
                    j = 8 * a + b - first
                    if 0 <= j < CONV_WIDTH:
                        term = w_ref[j:j + 1, cs] * pad_ref[r0 + 8 * a:r0 + 8 * a + ext, cs]
                        part = term if part is None else part + term
                if b:
                    part = pltpu.roll(part, ext - b, axis=0)[:rows]
                acc = acc + part
            dw_ref[r0:r0 + rows, cs] = acc

    dw = dw_ref[...]
    mu = jnp.mean(dw, axis=-1, keepdims=True)
    cen = dw - mu
    var = jnp.mean(cen * cen, axis=-1, keepdims=True)
    y = cen * lax.rsqrt(var + LN_EPS) * lng_ref[...] + lnb_ref[...]
    c_ref[...] = (y * jax.nn.sigmoid(y)).astype(_BF16)


def _conv_branch(glu, halo_src, tm, n_tiles, halo_map, zero_first_halo,
                 conv_dw, conv_dw_b, conv_ln_g, conv_ln_b, layer):
    dc = glu.shape[1]
    vec = _resident((None, 1, dc), lambda i: (layer, 0, 0))
    return pl.pallas_call(
        functools.partial(_conv_kernel, zero_first_halo=zero_first_halo),
        grid=(n_tiles,),
        in_specs=[
            pl.BlockSpec((HALO_ROWS, dc), halo_map),
            pl.BlockSpec((tm, dc), lambda i: (i, 0)),
            _resident((None, CONV_WIDTH, dc), lambda i: (layer, 0, 0)),
            vec, vec, vec,
        ],
        out_specs=pl.BlockSpec((tm, dc), lambda i: (i, 0)),
        out_shape=jax.ShapeDtypeStruct((n_tiles * tm, dc), _BF16),
        scratch_shapes=[pltpu.VMEM((tm + HALO_ROWS, dc), _F32), pltpu.VMEM((tm, dc), _F32)],
        compiler_params=_params("parallel"),
        name="conv_branch",
    )(halo_src, glu, conv_dw, conv_dw_b, conv_ln_g, conv_ln_b)


def _t5_bucket(rel):
    half = N_BUCKETS // 2
    max_exact = half // 2
    ret = jnp.where(rel > 0, half, 0)
    n = jnp.abs(rel)
    nf = jnp.maximum(n, 1).astype(jnp.float32)
    large = max_exact + (jnp.log(nf / max_exact) / math.log(MAX_DISTANCE / max_exact)
                         * (half - max_exact)).astype(jnp.int32)
    large = jnp.minimum(large, half - 1)
    return ret + jnp.where(n < max_exact, n, large)


def _bias_from_buckets(bucket, rb_ref, h):
    out = jnp.zeros(bucket.shape, _F32)
    for b in range(N_BUCKETS):
        out = jnp.where(bucket == b, rb_ref[b, h] * LOG2E, out)
    return out


def _reduce_rows(x, op):
    slab = 8
    parts = [x[r:r + slab] for r in range(0, x.shape[0], slab)]
    while len(parts) > 1:
        nxt = [op(parts[a], parts[a + 1]) for a in range(0, len(parts) - 1, 2)]
        if len(parts) % 2:
            nxt.append(parts[-1])
        parts = nxt
    red = jnp.max if op is jnp.maximum else jnp.sum
    return red(parts[0], axis=0, keepdims=True)


def _lambda(lq1_ref, lk1_ref, lq2_ref, lk2_ref, lambda_init):
    s1 = jnp.sum(lq1_ref[...] * lk1_ref[...], axis=-1, keepdims=True)
    s2 = jnp.sum(lq2_ref[...] * lk2_ref[...], axis=-1, keepdims=True)
    return jnp.exp(s1) - jnp.exp(s2) + lambda_init


def _prompt_attn_kernel(rb_ref, lq1_ref, lk1_ref, lq2_ref, lk2_ref, q_ref, k_ref, v_ref,
                        bucket_ref, g_ref, o_ref,
                        qq_ref, vt_ref, bias_ref, s_ref, smax_ref, m_ref, l_ref, acc_ref,
                        *, lambda_init):
    n_blocks, _, bq = vt_ref.shape
    h = pl.program_id(0)

    def block_rows(j):
        return pl.ds(pl.multiple_of(j * bq, bq), bq)

    bias_ref[0] = _bias_from_buckets(bucket_ref[0], rb_ref, h)
    kc = lax.broadcasted_iota(jnp.int32, (bq, bq), 0) // CHUNK
    qc = lax.broadcasted_iota(jnp.int32, (bq, bq), 1) // CHUNK
    bias_ref[1] = jnp.where(kc <= qc, _bias_from_buckets(bucket_ref[1], rb_ref, h), -jnp.inf)

    def transpose_block(jb, carry):
        vt_ref[jb] = v_ref[block_rows(jb), :].astype(_F32).T.astype(_BF16)
        return carry

    lax.fori_loop(0, n_blocks, transpose_block, 0)

    tiles = list(range(0, 2 * bq, SCORE_COLS))

    def issue_scores(j, bias_tile):
        k_blk = k_ref[block_rows(j), :]

        def issue(n):
            c0 = tiles[n]
            s = _dot(k_blk, qq_ref[:, c0:c0 + SCORE_COLS])
            if bias_tile is not None:
                q0 = c0 % bq
                s = s + bias_ref[bias_tile, :, q0:q0 + SCORE_COLS]
            s_ref[n] = s
            smax_ref[n] = _reduce_rows(s, jnp.maximum)

        return issue

    def block(j, shift, issue_next):
        vt = vt_ref[j]
        deferred = None
        for n, c0 in enumerate(tiles):
            cols = slice(c0, c0 + SCORE_COLS)
            m_old = m_ref[:, cols]
            m_new = jnp.maximum(m_old, smax_ref[n] + shift)
            p = jnp.exp2(s_ref[n] - (m_new - shift))
            alpha = jnp.exp2(m_old - m_new)
            l_ref[:, cols] = alpha * l_ref[:, cols] + _reduce_rows(p, jnp.add)
            m_ref[:, cols] = m_new
            pv = _dot(vt, p.astype(_BF16))
            if issue_next is not None:
                issue_next(n)
            if deferred is not None:
                dcols, dalpha, dpv = deferred
                acc_ref[:, dcols] = dalpha * acc_ref[:, dcols] + dpv
            deferred = (cols, alpha, pv)
        dcols, dalpha, dpv = deferred
        acc_ref[:, dcols] = dalpha * acc_ref[:, dcols] + dpv

    far_bias = rb_ref[FAR_BUCKET, h] * LOG2E
    lam = _lambda(lq1_ref, lk1_ref, lq2_ref, lk2_ref, lambda_init)

    def far_then_far(j):
        block(j, far_bias, issue_scores(j + 1, None))

    def query_block(i, carry):
        qt = q_ref[block_rows(i), :].astype(_F32).T.astype(_BF16)
        zeros = jnp.zeros((HEAD_DIM, bq), _BF16)
        qq_ref[0:HEAD_DIM, 0:bq] = qt[0:HEAD_DIM]
        qq_ref[HEAD_DIM:, 0:bq] = zeros
        qq_ref[0:HEAD_DIM, bq:] = zeros
        qq_ref[HEAD_DIM:, bq:] = qt[HEAD_DIM:]

        m_ref[...] = jnp.full(m_ref.shape, -jnp.inf, _F32)
        l_ref[...] = jnp.zeros(l_ref.shape, _F32)
        acc_ref[...] = jnp.zeros(acc_ref.shape, _F32)

        kind_of_first = jnp.minimum(i, 2)
        for kind, tile in ((0, 1), (1, 0), (2, None)):
            @pl.when(kind_of_first == kind)
            def _(tile=tile):
                issue = issue_scores(0, tile)
                for n in range(len(tiles)):
                    issue(n)

        first = 0
        left = jnp.maximum(i - 2, 0)
        for size in FAR_GROUPS:
            trips = left // size

            def far_group(t, c, size=size, first=first):
                for u in range(size):
                    far_then_far(first + size * t + u)
                return c

            lax.fori_loop(0, trips, far_group, 0)
            first = first + trips * size
            left = left - trips * size

        def finish():
            o = acc_ref[...] / l_ref[...]
            d = o[:, 0:bq] - lam * o[:, bq:]
            y = d * lax.rsqrt(jnp.mean(d * d, axis=0, keepdims=True) + SUBLN_EPS)
            o_ref[block_rows(i), :] = (y.T * g_ref[...] * (1.0 - lambda_init)).astype(_BF16)

        @pl.when(i >= 2)
        def _():
            block(i - 2, far_bias, issue_scores(i - 1, 0))
            block(i - 1, 0.0, issue_scores(i, 1))
            block(i, 0.0, None)
            finish()

        @pl.when(i == 1)
        def _():
            block(0, 0.0, issue_scores(1, 1))
            block(1, 0.0, None)
            finish()

        @pl.when(i == 0)
        def _():
            block(0, 0.0, None)
            finish()

        return carry

    lax.fori_loop(0, n_blocks, query_block, 0)


def _prompt_attention(q, kb, vb, rel_bias, lam_vecs, subln_g, layer, lambda_init):
    tp, d_attn = q.shape
    n_heads = d_attn // HEAD_COLS
    bq = _pick_tile(tp, (512, 256))
    assert bq + 1 >= MAX_DISTANCE and bq % CHUNK == 0 and bq % SCORE_COLS == 0
    nq = tp // bq
    pos = jnp.arange(bq, dtype=jnp.int32)
    bucket = jnp.stack([_t5_bucket(pos[:, None] - (pos[None, :] + bq)),
                        _t5_bucket(pos[:, None] - pos[None, :])])
    lam_spec = _resident((None, 1, HEAD_DIM), lambda h: (layer, 0, 0))
    head_cols = pl.BlockSpec((tp, HEAD_COLS), lambda h: (0, h))
    return pl.pallas_call(
        functools.partial(_prompt_attn_kernel, lambda_init=lambda_init),
        grid=(n_heads,),
        in_specs=[
            pl.BlockSpec(memory_space=pltpu.SMEM),
            lam_spec, lam_spec, lam_spec, lam_spec,
            head_cols, head_cols, head_cols,
            _resident((2, bq, bq), lambda h: (0, 0, 0)),
            _resident((None, 1, HEAD_COLS), lambda h: (layer, 0, 0)),
        ],
        out_specs=head_cols,
        out_shape=jax.ShapeDtypeStruct((tp, d_attn), _BF16),
        scratch_shapes=[
            pltpu.VMEM((HEAD_COLS, 2 * bq), _BF16),
            pltpu.VMEM((nq, HEAD_COLS, bq), _BF16),
            pltpu.VMEM((2, bq, bq), _F32),
            pltpu.VMEM((2 * bq // SCORE_COLS, bq, SCORE_COLS), _F32),
            pltpu.VMEM((2 * bq // SCORE_COLS, 1, SCORE_COLS), _F32),
            pltpu.VMEM((1, 2 * bq), _F32),
            pltpu.VMEM((1, 2 * bq), _F32),
            pltpu.VMEM((HEAD_COLS, 2 * bq), _F32),
        ],
        compiler_params=_params("parallel"),
        name="prompt_attention",
    )(rel_bias, *lam_vecs, q, kb, vb, bucket, subln_g)


def _sample_attn_kernel(rb_ref, lq1_ref, lk1_ref, lq2_ref, lk2_ref, q_ref, kn_ref, vn_ref,
                        kc_ref, vc_ref, bucket_c_ref, bucket_n_ref, g_ref, o_ref,
                        bias_c_ref, bias_n_ref, qq_ref, m_ref, l_ref, acc_ref, *, lambda_init):
    ts = q_ref.shape[0]
    n_heads = qq_ref.shape[0]
    chunk = kc_ref.shape[0] // n_heads
    b = pl.program_id(0)
    c = pl.program_id(1)

    @pl.when((b == 0) & (c == 0))
    def _():
        for h in range(n_heads):
            bias_n_ref[h] = _bias_from_buckets(bucket_n_ref[...], rb_ref, h)
            for cc in range(bias_c_ref.shape[1]):
                bias_c_ref[h, cc] = _bias_from_buckets(bucket_c_ref[cc], rb_ref, h)

    def update(h, s, v):
        m_old = m_ref[h]
        m_new = jnp.maximum(m_old, jnp.max(s, axis=-1, keepdims=True))
        p = jnp.exp2(s - m_new)
        alpha = jnp.exp2(m_old - m_new)
        l_ref[h] = alpha * l_ref[h] + jnp.sum(p, axis=-1, keepdims=True)
        acc_ref[h] = alpha * acc_ref[h] + _dot(p.astype(_BF16), v)
        m_ref[h] = m_new

    @pl.when(c == 0)
    def _():
        m_ref[...] = jnp.full(m_ref.shape, -jnp.inf, _F32)
        l_ref[...] = jnp.zeros(l_ref.shape, _F32)
        acc_ref[...] = jnp.zeros(acc_ref.shape, _F32)
        for h in range(n_heads):
            hc = slice(h * HEAD_COLS, (h + 1) * HEAD_COLS)
            q = q_ref[:, hc]
            lane = lax.broadcasted_iota(jnp.int32, q.shape, 1)
            zero = jnp.zeros_like(q)
            qq_ref[h] = jnp.concatenate([jnp.where(lane < HEAD_DIM, q, zero),
                                         jnp.where(lane >= HEAD_DIM, q, zero)], axis=0)
            bn = bias_n_ref[h]
            head_rows = pl.ds(h, ts, stride=n_heads)
            s = (_dot_nt(qq_ref[h], kn_ref[head_rows, :].astype(_BF16))
                 + jnp.concatenate([bn, bn], axis=0))
            update(h, s, vn_ref[head_rows, :].astype(_BF16))

    for h in range(n_heads):
        head_rows = pl.ds(h, chunk, stride=n_heads)
        bc = bias_c_ref[h, c]
        s = (_dot_nt(qq_ref[h], kc_ref[head_rows, :].astype(_BF16))
             + jnp.concatenate([bc, bc], axis=0))
        update(h, s, vc_ref[head_rows, :].astype(_BF16))

    @pl.when(c == pl.num_programs(1) - 1)
    def _():
        lam = _lambda(lq1_ref, lk1_ref, lq2_ref, lk2_ref, lambda_init)
        for h in range(n_heads):
            o = acc_ref[h] / l_ref[h]
            d = o[:ts] - lam * o[ts:]
            o_ref[:, h * HEAD_COLS:(h + 1) * HEAD_COLS] = (
                _rmsnorm(d, g_ref[...], SUBLN_EPS) * (1.0 - lambda_init)).astype(_BF16)


def _sample_attention(q, k, v, cache_k, cache_v, bs, ts, rel_bias, lam_vecs, subln_g,
                      layer, lambda_init):
    d_attn = q.shape[1]
    n_heads = d_attn // HEAD_COLS
    past = cache_k.shape[2] // n_heads
    assert ts % 16 == 0
    assert (past + ts - 1) // CHUNK <= past // CHUNK, "new frames must share the queries' chunk"
    chunk = _pick_tile(past, (1024, 512, 256, 128))
    n_chunks = past // chunk
    q_pos = past + jnp.arange(ts, dtype=jnp.int32)
    bucket_c = _t5_bucket(jnp.arange(past, dtype=jnp.int32)[None, :] - q_pos[:, None])
    bucket_c = bucket_c.reshape(ts, n_chunks, chunk).transpose(1, 0, 2)
    bucket_n = _t5_bucket(q_pos[None, :] - q_pos[:, None])
    new_q = pl.BlockSpec((ts, d_attn), lambda b, c: (b, 0))
    new_kv = pl.BlockSpec((ts * n_heads, HEAD_COLS), lambda b, c: (b, 0))
    cache = pl.BlockSpec((None, None, chunk * n_heads, HEAD_COLS), lambda b, c: (layer, b, c, 0))
    lam_spec = _resident((None, 1, HEAD_DIM), lambda b, c: (layer, 0, 0))
    return pl.pallas_call(
        functools.partial(_sample_attn_kernel, lambda_init=lambda_init),
        grid=(bs, n_chunks),
        in_specs=[
            pl.BlockSpec(memory_space=pltpu.SMEM),
            lam_spec, lam_spec, lam_spec, lam_spec,
            new_q, new_kv, new_kv, cache, cache,
            _resident((n_chunks, ts, chunk), lambda b, c: (0, 0, 0)),
            _resident((ts, ts), lambda b, c: (0, 0)),
            _resident((None, 1, HEAD_COLS), lambda b, c: (layer, 0, 0)),
        ],
        out_specs=pl.BlockSpec((ts, d_attn), lambda b, c: (b, 0)),
        out_shape=jax.ShapeDtypeStruct((bs * ts, d_attn), _BF16),
        scratch_shapes=[
            pltpu.VMEM((n_heads, n_chunks, ts, chunk), _F32),
            pltpu.VMEM((n_heads, ts, ts), _F32),
            pltpu.VMEM((n_heads, 2 * ts, HEAD_COLS), _BF16),
            pltpu.VMEM((n_heads, 2 * ts, 1), _F32),
            pltpu.VMEM((n_heads, 2 * ts, 1), _F32),
            pltpu.VMEM((n_heads, 2 * ts, HEAD_COLS), _F32),
        ],
        compiler_params=_params("arbitrary", "arbitrary"),
        name="sample_attention",
    )(rel_bias, *lam_vecs, q, k, v, cache_k, cache_v, bucket_c, bucket_n, subln_g)


def _merge_kernel(x_ref, c_ref, on_ref, g_ref, wga_ref, wgb_ref, wco_ref, wao_ref, wout_ref, y_ref):
    x = x_ref[...]
    h = _rmsnorm(x, g_ref[...], RMS_EPS).astype(_BF16)
    gate_a = jax.nn.sigmoid(_dot(h, wga_ref[...]))
    gate_b = jax.nn.sigmoid(_dot(h, wgb_ref[...]))
    merged = gate_a * _dot(c_ref[...], wco_ref[...]) + gate_b * _dot(on_ref[...], wao_ref[...])
    y_ref[...] = x + _dot(merged.astype(_BF16), wout_ref[...])


def _merge(x, c, o_n, norm_g, w_in_b, w_conv_out_b, w_attn_o_b, w_out_b, layer):
    r, d = x.shape
    tm = _pick_tile(r, (640, 512, 320, 256, 128, 64, 32, 16))
    gate_col0 = w_in_b.shape[2] // d - 2
    row = pl.BlockSpec((tm, d), lambda i: (i, 0))
    sq = _resident((None, d, d), lambda i: (layer, 0, 0))
    return pl.pallas_call(
        _merge_kernel,
        grid=(r // tm,),
        in_specs=[
            row, row, row,
            _resident((None, 1, d), lambda i: (layer, 0, 0)),
            _resident((None, d, d), lambda i: (layer, 0, gate_col0)),
            _resident((None, d, d), lambda i: (layer, 0, gate_col0 + 1)),
            sq, sq, sq,
        ],
        out_specs=row,
        out_shape=jax.ShapeDtypeStruct((r, d), _F32),
        compiler_params=_params("parallel"),
        name="merge_out_proj",
    )(x, c, o_n, norm_g, w_in_b, w_in_b, w_conv_out_b, w_attn_o_b, w_out_b)


def _ffn_kernel(x_ref, g_ref, wfi_ref, wfo_ref, gf_ref, y_ref, *, final_norm):
    d_ff = wfo_ref.shape[0]
    x = x_ref[...]
    h = _rmsnorm(x, g_ref[...], RMS_EPS).astype(_BF16)
    y = x
    half = (d_ff // 2) // 256 * 256
    for c0, c1 in ((0, half), (half, d_ff)):
        gate = _dot(h, wfi_ref[:, c0:c1])
        up = _dot(h, wfi_ref[:, d_ff + c0:d_ff + c1])
        act = (gate * jax.nn.sigmoid(gate) * up).astype(_BF16)
        y = y + _dot(act, wfo_ref[c0:c1, :])
    if final_norm:
        y = _rmsnorm(y, gf_ref[...], RMS_EPS)
    y_ref[...] = y


def _ffn(x, norm_g, w_ffn_in_b, w_ffn_out_b, norm_final, layer, final_norm):
    r, d = x.shape
    d_ff = w_ffn_out_b.shape[1]
    tm = _pick_tile(r, (640, 512, 320, 256, 128, 64, 32, 16))
    row = pl.BlockSpec((tm, d), lambda i: (i, 0))
    return pl.pallas_call(
        functools.partial(_ffn_kernel, final_norm=final_norm),
        grid=(r // tm,),
        in_specs=[
            row,
            _resident((None, 1, d), lambda i: (layer, 0, 0)),
            _resident((None, d, 2 * d_ff), lambda i: (layer, 0, 0)),
            _resident((None, d_ff, d), lambda i: (layer, 0, 0)),
            _resident((1, d), lambda i: (0, 0)),
        ],
        out_specs=row,
        out_shape=jax.ShapeDtypeStruct((r, d), _F32),
        compiler_params=_params("parallel"),
        name="ffn",
    )(x, norm_g, w_ffn_in_b, w_ffn_out_b, norm_final)


def kernel(x_prompt, x_sample, cache_k, cache_v, state_conv, norm_mix, w_in, conv_dw, conv_dw_b,
           conv_ln_g, conv_ln_b, w_conv_out, lambda_q1, lambda_k1, lambda_q2, lambda_k2,
           attn_subln_g, w_attn_o, w_out, norm_ffn, w_ffn_in, w_ffn_out, rel_bias, norm_final):
    depth = w_in.shape[0]
    bp, tp, d = x_prompt.shape
    bs, ts, _ = x_sample.shape
    past = cache_k.shape[2]
    n_heads = cache_k.shape[3]
    assert bp == 1, "one new stream per step"
    assert n_heads == N_HEADS and cache_k.shape[4] == HEAD_COLS
    assert rel_bias.shape == (N_BUCKETS, N_HEADS)
    assert ts >= CONV_STATE and ts % HALO_ROWS == 0 and tp % ts == 0
    dc = conv_dw.shape[2]

    xp = x_prompt.reshape(tp, d)
    xs = x_sample.reshape(bs * ts, d)

    as_rows = lambda a: a.reshape(depth, 1, a.shape[-1])
    norm_mix3, norm_ffn3 = as_rows(norm_mix), as_rows(norm_ffn)
    conv_b3, ln_g3, ln_b3 = as_rows(conv_dw_b), as_rows(conv_ln_g), as_rows(conv_ln_b)
    subln3 = as_rows(attn_subln_g)
    lam3 = [as_rows(a) for a in (lambda_q1, lambda_k1, lambda_q2, lambda_k2)]
    norm_final2 = norm_final.reshape(1, d)
    rel_bias = rel_bias.astype(_F32)
    w_in_b = w_in.astype(_BF16)
    w_conv_out_b = w_conv_out.astype(_BF16)
    w_attn_o_b = w_attn_o.astype(_BF16)
    w_out_b = w_out.astype(_BF16)
    w_ffn_in_b = w_ffn_in.astype(_BF16)
    w_ffn_out_b = w_ffn_out.astype(_BF16)
    cache_k4 = cache_k.reshape(depth, bs, past * n_heads, HEAD_COLS)
    cache_v4 = cache_v.reshape(depth, bs, past * n_heads, HEAD_COLS)
    halo_s = jnp.pad(state_conv, ((0, 0), (0, 0), (HALO_ROWS - CONV_STATE, 0), (0, 0)))
    halo_s = halo_s.reshape(depth, bs * HALO_ROWS, dc)

    tm_conv = _pick_tile(tp, (128, 64, 32))
    halo_per_tile = tm_conv // HALO_ROWS

    k_all = jnp.zeros((depth * tp * n_heads, HEAD_COLS), _F32)
    v_all = jnp.zeros((depth * tp * n_heads, HEAD_COLS), _F32)
    c_p, k_s, v_s, c_s = [], [], [], []
    for layer in range(depth):
        lambda_init = 0.8 - 0.6 * math.exp(-0.3 * layer)
        last = layer == depth - 1
        conv_args = (conv_dw, conv_b3, ln_g3, ln_b3, layer)
        merge_args = (norm_mix3, w_in_b, w_conv_out_b, w_attn_o_b, w_out_b, layer)
        ffn_args = (norm_ffn3, w_ffn_in_b, w_ffn_out_b, norm_final2, layer, last)

        glu, q, k_all, v_all, kb, vb = _in_proj(xp, norm_mix3, w_in_b, layer, (k_all, v_all))
        c = _conv_branch(glu, glu, tm_conv, tp // tm_conv,
                         lambda i: (jnp.maximum(i * halo_per_tile - 1, 0), 0), True, *conv_args)
        o_n = _prompt_attention(q, kb, vb, rel_bias, lam3, subln3, layer, lambda_init)
        xp = _ffn(_merge(xp, c, o_n, *merge_args), *ffn_args)
        c_p.append(glu[tp - CONV_STATE:].reshape(1, CONV_STATE, dc))

        glu, q, k, v, _, _ = _in_proj(xs, norm_mix3, w_in_b, layer)
        c = _conv_branch(glu, halo_s[layer], ts, bs, lambda i: (i, 0), False, *conv_args)
        o_n = _sample_attention(q, k, v, cache_k4, cache_v4, bs, ts, rel_bias, lam3, subln3,
                                layer, lambda_init)
        xs = _ffn(_merge(xs, c, o_n, *merge_args), *ffn_args)
        k_s.append(k.reshape(bs, ts, n_heads, HEAD_COLS))
        v_s.append(v.reshape(bs, ts, n_heads, HEAD_COLS))
        c_s.append(glu.reshape(bs, ts, dc)[:, ts - CONV_STATE:])

    kv_shape = (depth, 1, tp, n_heads, HEAD_COLS)
    return (xp.reshape(1, tp, d), xs.reshape(bs, ts, d), k_all.reshape(kv_shape),
            v_all.reshape(kv_shape), jnp.stack(c_p), jnp.stack(k_s), jnp.stack(v_s),
            jnp.stack(c_s))
```

```python
import functools
import math

import jax
import jax.numpy as jnp
from jax import lax
from jax.experimental import pallas as pl
from jax.experimental.pallas import tpu as pltpu

N_HEADS = 8
HEAD_DIM = 64
HEAD_COLS = 2 * HEAD_DIM
CONV_WIDTH = 31
CONV_STATE = CONV_WIDTH - 1
CHUNK = 64
N_BUCKETS = 32
MAX_DISTANCE = 128
FAR_BUCKET = N_BUCKETS // 2 - 1
RMS_EPS = 1e-6
LN_EPS = 1e-5
SUBLN_EPS = 1e-5
LOG2E = math.log2(math.e)

HALO_ROWS = 32
SCORE_COLS = 256
FAR_GROUPS = (8, 4, 2, 1)
V7X_VMEM_LIMIT_BYTES = 56 * 1024 * 1024

_F32 = jnp.float32
_BF16 = jnp.bfloat16


def _pick_tile(n, candidates):
    for c in candidates:
        if n % c == 0:
            return c
    raise ValueError(f"no tile in {candidates} divides {n}")


def _rmsnorm(x, g, eps):
    return x * lax.rsqrt(jnp.mean(x * x, axis=-1, keepdims=True) + eps) * g


def _dot(a, b):
    return jnp.dot(a, b, preferred_element_type=_F32)


def _dot_nt(a, b):
    return lax.dot_general(a, b, (((1,), (1,)), ((), ())), preferred_element_type=_F32)


def _params(*semantics):
    return pltpu.CompilerParams(dimension_semantics=semantics,
                                vmem_limit_bytes=V7X_VMEM_LIMIT_BYTES)


def _resident(block_shape, index_map):
    return pl.BlockSpec(block_shape, index_map, pipeline_mode=pl.Buffered(1))


def _in_proj_kernel(x_ref, g_ref, w_ref, *refs, transposed_qv):
    glu_ref, q_ref, k_ref, v_ref, kb_ref, vb_ref = refs[-6:]
    d = x_ref.shape[1]
    h = _rmsnorm(x_ref[...], g_ref[...], RMS_EPS).astype(_BF16)

    def proj(c):
        return _dot(h, w_ref[:, c * d:(c + 1) * d])

    def bf16_tile(y, transposed):
        return y.T.astype(_BF16) if transposed else y.astype(_BF16)

    glu_ref[...] = proj(0) * jax.nn.sigmoid(proj(1))
    q_ref[...] = bf16_tile(proj(2) * (HEAD_DIM ** -0.5 * LOG2E), transposed_qv)
    tm = x_ref.shape[0]
    for full_ref, bf16_ref, col in ((k_ref, kb_ref, 3), (v_ref, vb_ref, 4)):
        y = proj(col)
        bf16_ref[...] = bf16_tile(y, transposed_qv and col == 4)
        for head in range(N_HEADS):
            full_ref[pl.ds(head, tm, stride=N_HEADS), :] = (
                y[:, head * HEAD_COLS:(head + 1) * HEAD_COLS])


def _in_proj(x, norm_g, w_in_b, layer, kv_stacked=None, block_t=None):
    r, d = x.shape
    assert d == N_HEADS * HEAD_COLS
    tm = block_t or _pick_tile(r, (512, 256, 128, 64, 32, 16))
    n_tiles = r // tm
    row = pl.BlockSpec((tm, d), lambda i: (i, 0))
    f32_out = jax.ShapeDtypeStruct((r, d), _F32)
    bf16_out = jax.ShapeDtypeStruct((r, d), _BF16)
    if block_t:
        row_t = pl.BlockSpec((None, d, tm), lambda i: (i, 0, 0))
        bf16_t_out = jax.ShapeDtypeStruct((n_tiles, d, tm), _BF16)
    else:
        row_t, bf16_t_out = row, bf16_out
    in_specs = [
        row,
        _resident((None, 1, d), lambda i: (layer, 0, 0)),
        _resident((None, d, 5 * d), lambda i: (layer, 0, 0)),
    ]
    if kv_stacked is None:
        kv_stacked = ()
        aliases = {}
        tile0 = 0
        head_rows_out = jax.ShapeDtypeStruct((r * N_HEADS, HEAD_COLS), _F32)
    else:
        in_specs += [pl.BlockSpec(memory_space=pl.ANY)] * 2
        aliases = {3: 2, 4: 3}
        tile0 = layer * n_tiles
        head_rows_out = jax.ShapeDtypeStruct(kv_stacked[0].shape, _F32)
    head_rows = pl.BlockSpec((tm * N_HEADS, HEAD_COLS), lambda i: (tile0 + i, 0))
    return pl.pallas_call(
        functools.partial(_in_proj_kernel, transposed_qv=bool(block_t)),
        grid=(n_tiles,),
        in_specs=in_specs,
        out_specs=[row, row_t, head_rows, head_rows, row, row_t],
        out_shape=[f32_out, bf16_t_out, head_rows_out, head_rows_out, bf16_out, bf16_t_out],
        input_output_aliases=aliases,
        compiler_params=_params("parallel"),
        name="in_proj",
    )(x, norm_g, w_in_b, *kv_stacked)


def _conv_kernel(halo_ref, glu_ref, w_ref, b_ref, lng_ref, lnb_ref, c_ref, pad_ref, dw_ref,
                 *, zero_first_halo):
    tm, dc = glu_ref.shape
    halo = halo_ref[...]
    if zero_first_halo:
        halo = jnp.where(pl.program_id(0) == 0, jnp.zeros_like(halo), halo)
    pad_ref[0:HALO_ROWS, :] = halo
    pad_ref[HALO_ROWS:, :] = glu_ref[...]

    first = HALO_ROWS - CONV_STATE
    rows = _pick_tile(tm, (64, 32))
    lanes = 128
    for r0 in range(0, tm, rows):
        for c0 in range(0, dc, lanes):
            cs = slice(c0, c0 + lanes)
            acc = jnp.broadcast_to(b_ref[:, cs], (rows, lanes))
            for b in range(8):
                ext = rows + (8 if b else 0)
                part = None
                for a in range((first + CONV_WIDTH - 1) // 8 + 1):
                    j = 8 * a + b - first
                    if 0 <= j < CONV_WIDTH:
                        term = w_ref[j:j + 1, cs] * pad_ref[r0 + 8 * a:r0 + 8 * a + ext, cs]
                        part = term if part is None else part + term
                if b:
                    part = pltpu.roll(part, ext - b, axis=0)[:rows]
                acc = acc + part
            dw_ref[r0:r0 + rows, cs] = acc

    dw = dw_ref[...]
    mu = jnp.mean(dw, axis=-1, keepdims=True)
    cen = dw - mu
    var = jnp.mean(cen * cen, axis=-1, keepdims=True)
    y = cen * lax.rsqrt(var + LN_EPS) * lng_ref[...] + lnb_ref[...]
    c_ref[...] = (y * jax.nn.sigmoid(y)).astype(_BF16)


def _conv_branch(glu, halo_src, tm, n_tiles, halo_map, zero_first_halo,
                 conv_dw, conv_dw_b, conv_ln_g, conv_ln_b, layer):
    dc = glu.shape[1]
    vec = _resident((None, 1, dc), lambda i: (layer, 0, 0))
    return pl.pallas_call(
        functools.partial(_conv_kernel, zero_first_halo=zero_first_halo),
        grid=(n_tiles,),
        in_specs=[
            pl.BlockSpec((HALO_ROWS, dc), halo_map),
            pl.BlockSpec((tm, dc), lambda i: (i, 0)),
            _resident((None, CONV_WIDTH, dc), lambda i: (layer, 0, 0)),
            vec, vec, vec,
        ],
        out_specs=pl.BlockSpec((tm, dc), lambda i: (i, 0)),
        out_shape=jax.ShapeDtypeStruct((n_tiles * tm, dc), _BF16),
        scratch_shapes=[pltpu.VMEM((tm + HALO_ROWS, dc), _F32), pltpu.VMEM((tm, dc), _F32)],
        compiler_params=_params("parallel"),
        name="conv_branch",
    )(halo_src, glu, conv_dw, conv_dw_b, conv_ln_g, conv_ln_b)


def _t5_bucket(rel):
    half = N_BUCKETS // 2
    max_exact = half // 2
    ret = jnp.where(rel > 0, half, 0)
    n = jnp.abs(rel)
    nf = jnp.maximum(n, 1).astype(jnp.float32)
    large = max_exact + (jnp.log(nf / max_exact) / math.log(MAX_DISTANCE / max_exact)
                         * (half - max_exact)).astype(jnp.int32)
    large = jnp.minimum(large, half - 1)
    return ret + jnp.where(n < max_exact, n, large)


def _bias_from_buckets(bucket, rb_ref, h):
    out = jnp.zeros(bucket.shape, _F32)
    for b in range(N_BUCKETS):
        out = jnp.where(bucket == b, rb_ref[b, h] * LOG2E, out)
    return out


def _reduce_rows(x, op):
    slab = 8
    parts = [x[r:r + slab] for r in range(0, x.shape[0], slab)]
    while len(parts) > 1:
        nxt = [op(parts[a], parts[a + 1]) for a in range(0, len(parts) - 1, 2)]
        if len(parts) % 2:
            nxt.append(parts[-1])
        parts = nxt
    red = jnp.max if op is jnp.maximum else jnp.sum
    return red(parts[0], axis=0, keepdims=True)


def _lambda(lq1_ref, lk1_ref, lq2_ref, lk2_ref, lambda_init):
    s1 = jnp.sum(lq1_ref[...] * lk1_ref[...], axis=-1, keepdims=True)
    s2 = jnp.sum(lq2_ref[...] * lk2_ref[...], axis=-1, keepdims=True)
    return jnp.exp(s1) - jnp.exp(s2) + lambda_init


def _bias_tiles_kernel(rb_ref, bucket_ref, bias_ref):
    h = pl.program_id(0)
    bq = bias_ref.shape[1]
    bias_ref[0] = _bias_from_buckets(bucket_ref[0], rb_ref, h)
    kc = lax.broadcasted_iota(jnp.int32, (bq, bq), 0) // CHUNK
    qc = lax.broadcasted_iota(jnp.int32, (bq, bq), 1) // CHUNK
    bias_ref[1] = jnp.where(kc <= qc, _bias_from_buckets(bucket_ref[1], rb_ref, h), -jnp.inf)


def _bias_tiles(rel_bias, bq):
    n_heads = rel_bias.shape[1]
    pos = jnp.arange(bq, dtype=jnp.int32)
    bucket = jnp.stack([_t5_bucket(pos[:, None] - (pos[None, :] + bq)),
                        _t5_bucket(pos[:, None] - pos[None, :])])
    return pl.pallas_call(
        _bias_tiles_kernel,
        grid=(n_heads,),
        in_specs=[pl.BlockSpec(memory_space=pltpu.SMEM),
                  _resident((2, bq, bq), lambda h: (0, 0, 0))],
        out_specs=pl.BlockSpec((None, 2, bq, bq), lambda h: (h, 0, 0, 0)),
        out_shape=jax.ShapeDtypeStruct((n_heads, 2, bq, bq), _F32),
        compiler_params=_params("parallel"),
        name="bias_tiles",
    )(rel_bias, bucket)


def _prompt_attn_kernel(rb_ref, lq1_ref, lk1_ref, lq2_ref, lk2_ref, q_ref, k_ref, vt_ref,
                        bias_ref, g_ref, o_ref,
                        qq_ref, s_ref, smax_ref, m_ref, l_ref, acc_ref, *, lambda_init):
    n_blocks, _, bq = vt_ref.shape
    h = pl.program_id(0)

    def block_rows(j):
        return pl.ds(pl.multiple_of(j * bq, bq), bq)

    tiles = list(range(0, 2 * bq, SCORE_COLS))

    def issue_scores(j, bias_tile):
        k_blk = k_ref[block_rows(j), :]

        def issue(n):
            c0 = tiles[n]
            s = _dot(k_blk, qq_ref[:, c0:c0 + SCORE_COLS])
            if bias_tile is not None:
                q0 = c0 % bq
                s = s + bias_ref[bias_tile, :, q0:q0 + SCORE_COLS]
            s_ref[n] = s
            smax_ref[n] = _reduce_rows(s, jnp.maximum)

        return issue

    def block(j, shift, issue_next):
        vt = vt_ref[j]
        deferred = None
        for n, c0 in enumerate(tiles):
            cols = slice(c0, c0 + SCORE_COLS)
            m_old = m_ref[:, cols]
            m_new = jnp.maximum(m_old, smax_ref[n] + shift)
            p = jnp.exp2(s_ref[n] - (m_new - shift))
            alpha = jnp.exp2(m_old - m_new)
            l_ref[:, cols] = alpha * l_ref[:, cols] + _reduce_rows(p, jnp.add)
            m_ref[:, cols] = m_new
            pv = _dot(vt, p.astype(_BF16))
            if issue_next is not None:
                issue_next(n)
            if deferred is not None:
                dcols, dalpha, dpv = deferred
                acc_ref[:, dcols] = dalpha * acc_ref[:, dcols] + dpv
            deferred = (cols, alpha, pv)
        dcols, dalpha, dpv = deferred
        acc_ref[:, dcols] = dalpha * acc_ref[:, dcols] + dpv

    far_bias = rb_ref[FAR_BUCKET, h] * LOG2E
    lam = _lambda(lq1_ref, lk1_ref, lq2_ref, lk2_ref, lambda_init)

    def far_then_far(j):
        block(j, far_bias, issue_scores(j + 1, None))

    def query_block(i, carry):
        qt = q_ref[i]
        zeros = jnp.zeros((HEAD_DIM, bq), _BF16)
        qq_ref[0:HEAD_DIM, 0:bq] = qt[0:HEAD_DIM]
        qq_ref[HEAD_DIM:, 0:bq] = zeros
        qq_ref[0:HEAD_DIM, bq:] = zeros
        qq_ref[HEAD_DIM:, bq:] = qt[HEAD_DIM:]

        m_ref[...] = jnp.full(m_ref.shape, -jnp.inf, _F32)
        l_ref[...] = jnp.zeros(l_ref.shape, _F32)
        acc_ref[...] = jnp.zeros(acc_ref.shape, _F32)

        kind_of_first = jnp.minimum(i, 2)
        for kind, tile in ((0, 1), (1, 0), (2, None)):
            @pl.when(kind_of_first == kind)
            def _(tile=tile):
                issue = issue_scores(0, tile)
                for n in range(len(tiles)):
                    issue(n)

        first = 0
        left = jnp.maximum(i - 2, 0)
        for size in FAR_GROUPS:
            trips = left // size

            def far_group(t, c, size=size, first=first):
                for u in range(size):
                    far_then_far(first + size * t + u)
                return c

            lax.fori_loop(0, trips, far_group, 0)
            first = first + trips * size
            left = left - trips * size

        def finish():
            o = acc_ref[...] / l_ref[...]
            d = o[:, 0:bq] - lam * o[:, bq:]
            y = d * lax.rsqrt(jnp.mean(d * d, axis=0, keepdims=True) + SUBLN_EPS)
            o_ref[block_rows(i), :] = (y.T * g_ref[...] * (1.0 - lambda_init)).astype(_BF16)

        @pl.when(i >= 2)
        def _():
            block(i - 2, far_bias, issue_scores(i - 1, 0))
            block(i - 1, 0.0, issue_scores(i, 1))
            block(i, 0.0, None)
            finish()

        @pl.when(i == 1)
        def _():
            block(0, 0.0, issue_scores(1, 1))
            block(1, 0.0, None)
            finish()

        @pl.when(i == 0)
        def _():
            block(0, 0.0, None)
            finish()

        return carry

    lax.fori_loop(0, n_blocks, query_block, 0)


def _attn_block(tp):
    bq = _pick_tile(tp, (512, 256))
    assert bq + 1 >= MAX_DISTANCE and bq % CHUNK == 0 and bq % SCORE_COLS == 0
    return bq


def _prompt_attention(q_t, kb, v_t, bias, rel_bias, lam_vecs, subln_g, layer, lambda_init):
    nq, d_attn, bq = q_t.shape
    tp = nq * bq
    n_heads = d_attn // HEAD_COLS
    lam_spec = _resident((None, 1, HEAD_DIM), lambda h: (layer, 0, 0))
    head_cols = pl.BlockSpec((tp, HEAD_COLS), lambda h: (0, h))
    head_cols_t = pl.BlockSpec((nq, HEAD_COLS, bq), lambda h: (0, h, 0))
    return pl.pallas_call(
        functools.partial(_prompt_attn_kernel, lambda_init=lambda_init),
        grid=(n_heads,),
        in_specs=[
            pl.BlockSpec(memory_space=pltpu.SMEM),
            lam_spec, lam_spec, lam_spec, lam_spec,
            head_cols_t, head_cols, head_cols_t,
            pl.BlockSpec((None, 2, bq, bq), lambda h: (h, 0, 0, 0)),
            _resident((None, 1, HEAD_COLS), lambda h: (layer, 0, 0)),
        ],
        out_specs=head_cols,
        out_shape=jax.ShapeDtypeStruct((tp, d_attn), _BF16),
        scratch_shapes=[
            pltpu.VMEM((HEAD_COLS, 2 * bq), _BF16),
            pltpu.VMEM((2 * bq // SCORE_COLS, bq, SCORE_COLS), _F32),
            pltpu.VMEM((2 * bq // SCORE_COLS, 1, SCORE_COLS), _F32),
            pltpu.VMEM((1, 2 * bq), _F32),
            pltpu.VMEM((1, 2 * bq), _F32),
            pltpu.VMEM((HEAD_COLS, 2 * bq), _F32),
        ],
        compiler_params=_params("parallel"),
        name="prompt_attention",
    )(rel_bias, *lam_vecs, q_t, kb, v_t, bias, subln_g)


def _sample_attn_kernel(rb_ref, lq1_ref, lk1_ref, lq2_ref, lk2_ref, q_ref, kn_ref, vn_ref,
                        kc_ref, vc_ref, bucket_c_ref, bucket_n_ref, g_ref, o_ref,
                        bias_c_ref, bias_n_ref, qq_ref, m_ref, l_ref, acc_ref, *, lambda_init):
    ts = q_ref.shape[0]
    n_heads = qq_ref.shape[0]
    chunk = kc_ref.shape[0] // n_heads
    b = pl.program_id(0)
    c = pl.program_id(1)

    @pl.when((b == 0) & (c == 0))
    def _():
        for h in range(n_heads):
            bias_n_ref[h] = _bias_from_buckets(bucket_n_ref[...], rb_ref, h)
            for cc in range(bias_c_ref.shape[1]):
                bias_c_ref[h, cc] = _bias_from_buckets(bucket_c_ref[cc], rb_ref, h)

    def update(h, s, v):
        m_old = m_ref[h]
        m_new = jnp.maximum(m_old, jnp.max(s, axis=-1, keepdims=True))
        p = jnp.exp2(s - m_new)
        alpha = jnp.exp2(m_old - m_new)
        l_ref[h] = alpha * l_ref[h] + jnp.sum(p, axis=-1, keepdims=True)
        acc_ref[h] = alpha * acc_ref[h] + _dot(p.astype(_BF16), v)
        m_ref[h] = m_new

    @pl.when(c == 0)
    def _():
        m_ref[...] = jnp.full(m_ref.shape, -jnp.inf, _F32)
        l_ref[...] = jnp.zeros(l_ref.shape, _F32)
        acc_ref[...] = jnp.zeros(acc_ref.shape, _F32)
        for h in range(n_heads):
            hc = slice(h * HEAD_COLS, (h + 1) * HEAD_COLS)
            q = q_ref[:, hc]
            lane = lax.broadcasted_iota(jnp.int32, q.shape, 1)
            zero = jnp.zeros_like(q)
            qq_ref[h] = jnp.concatenate([jnp.where(lane < HEAD_DIM, q, zero),
                                         jnp.where(lane >= HEAD_DIM, q, zero)], axis=0)
            bn = bias_n_ref[h]
            head_rows = pl.ds(h, ts, stride=n_heads)
            s = (_dot_nt(qq_ref[h], kn_ref[head_rows, :].astype(_BF16))
                 + jnp.concatenate([bn, bn], axis=0))
            update(h, s, vn_ref[head_rows, :].astype(_BF16))

    for h in range(n_heads):
        head_rows = pl.ds(h, chunk, stride=n_heads)
        bc = bias_c_ref[h, c]
        s = (_dot_nt(qq_ref[h], kc_ref[head_rows, :].astype(_BF16))
             + jnp.concatenate([bc, bc], axis=0))
        update(h, s, vc_ref[head_rows, :].astype(_BF16))

    @pl.when(c == pl.num_programs(1) - 1)
    def _():
        lam = _lambda(lq1_ref, lk1_ref, lq2_ref, lk2_ref, lambda_init)
        for h in range(n_heads):
            o = acc_ref[h] / l_ref[h]
            d = o[:ts] - lam * o[ts:]
            o_ref[:, h * HEAD_COLS:(h + 1) * HEAD_COLS] = (
                _rmsnorm(d, g_ref[...], SUBLN_EPS) * (1.0 - lambda_init)).astype(_BF16)


def _sample_attention(q, k, v, cache_k, cache_v, bs, ts, rel_bias, lam_vecs, subln_g,
                      layer, lambda_init):
    d_attn = q.shape[1]
    n_heads = d_attn // HEAD_COLS
    past = cache_k.shape[2] // n_heads
    assert ts % 16 == 0
    assert (past + ts - 1) // CHUNK <= past // CHUNK, "new frames must share the queries' chunk"
    chunk = _pick_tile(past, (1024, 512, 256, 128))
    n_chunks = past // chunk
    q_pos = past + jnp.arange(ts, dtype=jnp.int32)
    bucket_c = _t5_bucket(jnp.arange(past, dtype=jnp.int32)[None, :] - q_pos[:, None])
    bucket_c = bucket_c.reshape(ts, n_chunks, chunk).transpose(1, 0, 2)
    bucket_n = _t5_bucket(q_pos[None, :] - q_pos[:, None])
    new_q = pl.BlockSpec((ts, d_attn), lambda b, c: (b, 0))
    new_kv = pl.BlockSpec((ts * n_heads, HEAD_COLS), lambda b, c: (b, 0))
    cache = pl.BlockSpec((None, None, chunk * n_heads, HEAD_COLS), lambda b, c: (layer, b, c, 0))
    lam_spec = _resident((None, 1, HEAD_DIM), lambda b, c: (layer, 0, 0))
    return pl.pallas_call(
        functools.partial(_sample_attn_kernel, lambda_init=lambda_init),
        grid=(bs, n_chunks),
        in_specs=[
            pl.BlockSpec(memory_space=pltpu.SMEM),
            lam_spec, lam_spec, lam_spec, lam_spec,
            new_q, new_kv, new_kv, cache, cache,
            _resident((n_chunks, ts, chunk), lambda b, c: (0, 0, 0)),
            _resident((ts, ts), lambda b, c: (0, 0)),
            _resident((None, 1, HEAD_COLS), lambda b, c: (layer, 0, 0)),
        ],
        out_specs=pl.BlockSpec((ts, d_attn), lambda b, c: (b, 0)),
        out_shape=jax.ShapeDtypeStruct((bs * ts, d_attn), _BF16),
        scratch_shapes=[
            pltpu.VMEM((n_heads, n_chunks, ts, chunk), _F32),
            pltpu.VMEM((n_heads, ts, ts), _F32),
            pltpu.VMEM((n_heads, 2 * ts, HEAD_COLS), _BF16),
            pltpu.VMEM((n_heads, 2 * ts, 1), _F32),
            pltpu.VMEM((n_heads, 2 * ts, 1), _F32),
            pltpu.VMEM((n_heads, 2 * ts, HEAD_COLS), _F32),
        ],
        compiler_params=_params("arbitrary", "arbitrary"),
        name="sample_attention",
    )(rel_bias, *lam_vecs, q, k, v, cache_k, cache_v, bucket_c, bucket_n, subln_g)


def _merge_kernel(x_ref, c_ref, on_ref, g_ref, wga_ref, wgb_ref, wco_ref, wao_ref, wout_ref, y_ref):
    x = x_ref[...]
    h = _rmsnorm(x, g_ref[...], RMS_EPS).astype(_BF16)
    gate_a = jax.nn.sigmoid(_dot(h, wga_ref[...]))
    gate_b = jax.nn.sigmoid(_dot(h, wgb_ref[...]))
    merged = gate_a * _dot(c_ref[...], wco_ref[...]) + gate_b * _dot(on_ref[...], wao_ref[...])
    y_ref[...] = x + _dot(merged.astype(_BF16), wout_ref[...])


def _merge(x, c, o_n, norm_g, w_in_b, w_conv_out_b, w_attn_o_b, w_out_b, layer):
    r, d = x.shape
    tm = _pick_tile(r, (640, 512, 320, 256, 128, 64, 32, 16))
    gate_col0 = w_in_b.shape[2] // d - 2
    row = pl.BlockSpec((tm, d), lambda i: (i, 0))
    sq = _resident((None, d, d), lambda i: (layer, 0, 0))
    return pl.pallas_call(
        _merge_kernel,
        grid=(r // tm,),
        in_specs=[
            row, row, row,
            _resident((None, 1, d), lambda i: (layer, 0, 0)),
            _resident((None, d, d), lambda i: (layer, 0, gate_col0)),
            _resident((None, d, d), lambda i: (layer, 0, gate_col0 + 1)),
            sq, sq, sq,
        ],
        out_specs=row,
        out_shape=jax.ShapeDtypeStruct((r, d), _F32),
        compiler_params=_params("parallel"),
        name="merge_out_proj",
    )(x, c, o_n, norm_g, w_in_b, w_in_b, w_conv_out_b, w_attn_o_b, w_out_b)


def _ffn_kernel(x_ref, g_ref, wfi_ref, wfo_ref, gf_ref, y_ref, *, final_norm):
    d_ff = wfo_ref.shape[0]
    x = x_ref[...]
    h = _rmsnorm(x, g_ref[...], RMS_EPS).astype(_BF16)
    y = x
    half = (d_ff // 2) // 256 * 256
    for c0, c1 in ((0, half), (half, d_ff)):
        gate = _dot(h, wfi_ref[:, c0:c1])
        up = _dot(h, wfi_ref[:, d_ff + c0:d_ff + c1])
        act = (gate * jax.nn.sigmoid(gate) * up).astype(_BF16)
        y = y + _dot(act, wfo_ref[c0:c1, :])
    if final_norm:
        y = _rmsnorm(y, gf_ref[...], RMS_EPS)
    y_ref[...] = y


def _ffn(x, norm_g, w_ffn_in_b, w_ffn_out_b, norm_final, layer, final_norm):
    r, d = x.shape
    d_ff = w_ffn_out_b.shape[1]
    tm = _pick_tile(r, (640, 512, 320, 256, 128, 64, 32, 16))
    row = pl.BlockSpec((tm, d), lambda i: (i, 0))
    return pl.pallas_call(
        functools.partial(_ffn_kernel, final_norm=final_norm),
        grid=(r // tm,),
        in_specs=[
            row,
            _resident((None, 1, d), lambda i: (layer, 0, 0)),
            _resident((None, d, 2 * d_ff), lambda i: (layer, 0, 0)),
            _resident((None, d_ff, d), lambda i: (layer, 0, 0)),
            _resident((1, d), lambda i: (0, 0)),
        ],
        out_specs=row,
        out_shape=jax.ShapeDtypeStruct((r, d), _F32),
        compiler_params=_params("parallel"),
        name="ffn",
    )(x, norm_g, w_ffn_in_b, w_ffn_out_b, norm_final)


def kernel(x_prompt, x_sample, cache_k, cache_v, state_conv, norm_mix, w_in, conv_dw, conv_dw_b,
           conv_ln_g, conv_ln_b, w_conv_out, lambda_q1, lambda_k1, lambda_q2, lambda_k2,
           attn_subln_g, w_attn_o, w_out, norm_ffn, w_ffn_in, w_ffn_out, rel_bias, norm_final):
    depth = w_in.shape[0]
    bp, tp, d = x_prompt.shape
    bs, ts, _ = x_sample.shape
    past = cache_k.shape[2]
    n_heads = cache_k.shape[3]
    assert bp == 1, "one new stream per step"
    assert n_heads == N_HEADS and cache_k.shape[4] == HEAD_COLS
    assert rel_bias.shape == (N_BUCKETS, N_HEADS)
    assert ts >= CONV_STATE and ts % HALO_ROWS == 0 and tp % ts == 0
    dc = conv_dw.shape[2]

    xp = x_prompt.reshape(tp, d)
    xs = x_sample.reshape(bs * ts, d)

    as_rows = lambda a: a.reshape(depth, 1, a.shape[-1])
    norm_mix3, norm_ffn3 = as_rows(norm_mix), as_rows(norm_ffn)
    conv_b3, ln_g3, ln_b3 = as_rows(conv_dw_b), as_rows(conv_ln_g), as_rows(conv_ln_b)
    subln3 = as_rows(attn_subln_g)
    lam3 = [as_rows(a) for a in (lambda_q1, lambda_k1, lambda_q2, lambda_k2)]
    norm_final2 = norm_final.reshape(1, d)
    rel_bias = rel_bias.astype(_F32)
    w_in_b = w_in.astype(_BF16)
    w_conv_out_b = w_conv_out.astype(_BF16)
    w_attn_o_b = w_attn_o.astype(_BF16)
    w_out_b = w_out.astype(_BF16)
    w_ffn_in_b = w_ffn_in.astype(_BF16)
    w_ffn_out_b = w_ffn_out.astype(_BF16)
    cache_k4 = cache_k.reshape(depth, bs, past * n_heads, HEAD_COLS)
    cache_v4 = cache_v.reshape(depth, bs, past * n_heads, HEAD_COLS)
    halo_s = jnp.pad(state_conv, ((0, 0), (0, 0), (HALO_ROWS - CONV_STATE, 0), (0, 0)))
    halo_s = halo_s.reshape(depth, bs * HALO_ROWS, dc)

    tm_conv = _pick_tile(tp, (128, 64, 32))
    halo_per_tile = tm_conv // HALO_ROWS
    bq = _attn_block(tp)
    attn_bias = _bias_tiles(rel_bias, bq)

    k_all = jnp.zeros((depth * tp * n_heads, HEAD_COLS), _F32)
    v_all = jnp.zeros((depth * tp * n_heads, HEAD_COLS), _F32)
    c_p, k_s, v_s, c_s = [], [], [], []
    for layer in range(depth):
        lambda_init = 0.8 - 0.6 * math.exp(-0.3 * layer)
        last = layer == depth - 1
        conv_args = (conv_dw, conv_b3, ln_g3, ln_b3, layer)
        merge_args = (norm_mix3, w_in_b, w_conv_out_b, w_attn_o_b, w_out_b, layer)
        ffn_args = (norm_ffn3, w_ffn_in_b, w_ffn_out_b, norm_final2, layer, last)

        glu, q_t, k_all, v_all, kb, v_t = _in_proj(xp, norm_mix3, w_in_b, layer, (k_all, v_all),
                                                  block_t=bq)
        c = _conv_branch(glu, glu, tm_conv, tp // tm_conv,
                         lambda i: (jnp.maximum(i * halo_per_tile - 1, 0), 0), True, *conv_args)
        o_n = _prompt_attention(q_t, kb, v_t, attn_bias, rel_bias, lam3, subln3, layer,
                                lambda_init)
        xp = _ffn(_merge(xp, c, o_n, *merge_args), *ffn_args)
        c_p.append(glu[tp - CONV_STATE:].reshape(1, CONV_STATE, dc))

        glu, q, k, v, _, _ = _in_proj(xs, norm_mix3, w_in_b, layer)
        c = _conv_branch(glu, halo_s[layer], ts, bs, lambda i: (i, 0), False, *conv_args)
        o_n = _sample_attention(q, k, v, cache_k4, cache_v4, bs, ts, rel_bias, lam3, subln3,
                                layer, lambda_init)
        xs = _ffn(_merge(xs, c, o_n, *merge_args), *ffn_args)
        k_s.append(k.reshape(bs, ts, n_heads, HEAD_COLS))
        v_s.append(v.reshape(bs, ts, n_heads, HEAD_COLS))
        c_s.append(glu.reshape(bs, ts, dc)[:, ts - CONV_STATE:])

    kv_shape = (depth, 1, tp, n_heads, HEAD_COLS)
    return (xp.reshape(1, tp, d), xs.reshape(bs, ts, d), k_all.reshape(kv_shape),
            v_all.reshape(kv_shape), jnp.stack(c_p), jnp.stack(k_s), jnp.stack(v_s),
            jnp.stack(c_s))
```

```python
import functools
import math

import jax
import jax.numpy as jnp
from jax import lax
from jax.experimental import pallas as pl
from jax.experimental.pallas import tpu as pltpu

N_HEADS = 8
HEAD_DIM = 64
HEAD_COLS = 2 * HEAD_DIM
CONV_WIDTH = 31
CONV_STATE = CONV_WIDTH - 1
CHUNK = 64
N_BUCKETS = 32
MAX_DISTANCE = 128
FAR_BUCKET = N_BUCKETS // 2 - 1
RMS_EPS = 1e-6
LN_EPS = 1e-5
SUBLN_EPS = 1e-5
LOG2E = math.log2(math.e)

HALO_ROWS = 32
SCORE_COLS = 256
FAR_GROUPS = (8, 4, 2, 1)
V7X_VMEM_LIMIT_BYTES = 56 * 1024 * 1024

_F32 = jnp.float32
_BF16 = jnp.bfloat16


def _pick_tile(n, candidates):
    for c in candidates:
        if n % c == 0:
            return c
    raise ValueError(f"no tile in {candidates} divides {n}")


def _rmsnorm(x, g, eps):
    return x * lax.rsqrt(jnp.mean(x * x, axis=-1, keepdims=True) + eps) * g


def _dot(a, b):
    return jnp.dot(a, b, preferred_element_type=_F32)


def _dot_nt(a, b):
    return lax.dot_general(a, b, (((1,), (1,)), ((), ())), preferred_element_type=_F32)


def _params(*semantics):
    return pltpu.CompilerParams(dimension_semantics=semantics,
                                vmem_limit_bytes=V7X_VMEM_LIMIT_BYTES)


def _resident(block_shape, index_map):
    return pl.BlockSpec(block_shape, index_map, pipeline_mode=pl.Buffered(1))


def _in_proj_kernel(x_ref, g_ref, w_ref, *refs, transposed_qv):
    glu_ref, q_ref, k_ref, v_ref, kb_ref, vb_ref = refs[-6:]
    d = x_ref.shape[1]
    h = _rmsnorm(x_ref[...], g_ref[...], RMS_EPS).astype(_BF16)

    def proj(c):
        return _dot(h, w_ref[:, c * d:(c + 1) * d])

    def bf16_tile(y, transposed):
        return y.T.astype(_BF16) if transposed else y.astype(_BF16)

    glu_ref[...] = proj(0) * jax.nn.sigmoid(proj(1))
    q_ref[...] = bf16_tile(proj(2) * (HEAD_DIM ** -0.5 * LOG2E), transposed_qv)
    tm = x_ref.shape[0]
    for full_ref, bf16_ref, col in ((k_ref, kb_ref, 3), (v_ref, vb_ref, 4)):
        y = proj(col)
        bf16_ref[...] = bf16_tile(y, transposed_qv and col == 4)
        for head in range(N_HEADS):
            full_ref[pl.ds(head, tm, stride=N_HEADS), :] = (
                y[:, head * HEAD_COLS:(head + 1) * HEAD_COLS])


def _in_proj(x, norm_g, w_in_b, layer, kv_stacked=None, block_t=None):
    r, d = x.shape
    assert d == N_HEADS * HEAD_COLS
    tm = block_t or _pick_tile(r, (512, 256, 128, 64, 32, 16))
    n_tiles = r // tm
    row = pl.BlockSpec((tm, d), lambda i: (i, 0))
    f32_out = jax.ShapeDtypeStruct((r, d), _F32)
    bf16_out = jax.ShapeDtypeStruct((r, d), _BF16)
    if block_t:
        row_t = pl.BlockSpec((None, d, tm), lambda i: (i, 0, 0))
        bf16_t_out = jax.ShapeDtypeStruct((n_tiles, d, tm), _BF16)
    else:
        row_t, bf16_t_out = row, bf16_out
    in_specs = [
        row,
        _resident((None, 1, d), lambda i: (layer, 0, 0)),
        _resident((None, d, 5 * d), lambda i: (layer, 0, 0)),
    ]
    if kv_stacked is None:
        kv_stacked = ()
        aliases = {}
        tile0 = 0
        head_rows_out = jax.ShapeDtypeStruct((r * N_HEADS, HEAD_COLS), _F32)
    else:
        in_specs += [pl.BlockSpec(memory_space=pl.ANY)] * 2
        aliases = {3: 2, 4: 3}
        tile0 = layer * n_tiles
        head_rows_out = jax.ShapeDtypeStruct(kv_stacked[0].shape, _F32)
    head_rows = pl.BlockSpec((tm * N_HEADS, HEAD_COLS), lambda i: (tile0 + i, 0))
    return pl.pallas_call(
        functools.partial(_in_proj_kernel, transposed_qv=bool(block_t)),
        grid=(n_tiles,),
        in_specs=in_specs,
        out_specs=[row, row_t, head_rows, head_rows, row, row_t],
        out_shape=[f32_out, bf16_t_out, head_rows_out, head_rows_out, bf16_out, bf16_t_out],
        input_output_aliases=aliases,
        compiler_params=_params("parallel"),
        name="in_proj",
    )(x, norm_g, w_in_b, *kv_stacked)


def _conv_kernel(halo_ref, glu_ref, w_ref, b_ref, lng_ref, lnb_ref, c_ref, pad_ref, dw_ref,
                 *, zero_first_halo):
    tm, dc = glu_ref.shape
    halo = halo_ref[...]
    if zero_first_halo:
        halo = jnp.where(pl.program_id(0) == 0, jnp.zeros_like(halo), halo)
    pad_ref[0:HALO_ROWS, :] = halo
    pad_ref[HALO_ROWS:, :] = glu_ref[...]

    first = HALO_ROWS - CONV_STATE
    rows = _pick_tile(tm, (64, 32))
    lanes = 128
    for r0 in range(0, tm, rows):
        for c0 in range(0, dc, lanes):
            cs = slice(c0, c0 + lanes)
            acc = jnp.broadcast_to(b_ref[:, cs], (rows, lanes))
            for b in range(8):
                ext = rows + (8 if b else 0)
                part = None
                for a in range((first + CONV_WIDTH - 1) // 8 + 1):
                    j = 8 * a + b - first
                    if 0 <= j < CONV_WIDTH:
                        term = w_ref[j:j + 1, cs] * pad_ref[r0 + 8 * a:r0 + 8 * a + ext, cs]
                        part = term if part is None else part + term
                if b:
                    part = pltpu.roll(part, ext - b, axis=0)[:rows]
                acc = acc + part
            dw_ref[r0:r0 + rows, cs] = acc

    dw = dw_ref[...]
    mu = jnp.mean(dw, axis=-1, keepdims=True)
    cen = dw - mu
    var = jnp.mean(cen * cen, axis=-1, keepdims=True)
    y = cen * lax.rsqrt(var + LN_EPS) * lng_ref[...] + lnb_ref[...]
    c_ref[...] = (y * jax.nn.sigmoid(y)).astype(_BF16)


def _conv_branch(glu, halo_src, tm, n_tiles, halo_map, zero_first_halo,
                 conv_dw, conv_dw_b, conv_ln_g, conv_ln_b, layer):
    dc = glu.shape[1]
    vec = _resident((None, 1, dc), lambda i: (layer, 0, 0))
    return pl.pallas_call(
        functools.partial(_conv_kernel, zero_first_halo=zero_first_halo),
        grid=(n_tiles,),
        in_specs=[
            pl.BlockSpec((HALO_ROWS, dc), halo_map),
            pl.BlockSpec((tm, dc), lambda i: (i, 0)),
            _resident((None, CONV_WIDTH, dc), lambda i: (layer, 0, 0)),
            vec, vec, vec,
        ],
        out_specs=pl.BlockSpec((tm, dc), lambda i: (i, 0)),
        out_shape=jax.ShapeDtypeStruct((n_tiles * tm, dc), _BF16),
        scratch_shapes=[pltpu.VMEM((tm + HALO_ROWS, dc), _F32), pltpu.VMEM((tm, dc), _F32)],
        compiler_params=_params("parallel"),
        name="conv_branch",
    )(halo_src, glu, conv_dw, conv_dw_b, conv_ln_g, conv_ln_b)


def _t5_bucket(rel):
    half = N_BUCKETS // 2
    max_exact = half // 2
    ret = jnp.where(rel > 0, half, 0)
    n = jnp.abs(rel)
    nf = jnp.maximum(n, 1).astype(jnp.float32)
    large = max_exact + (jnp.log(nf / max_exact) / math.log(MAX_DISTANCE / max_exact)
                         * (half - max_exact)).astype(jnp.int32)
    large = jnp.minimum(large, half - 1)
    return ret + jnp.where(n < max_exact, n, large)


def _bias_from_buckets(bucket, rb_ref, h):
    out = jnp.zeros(bucket.shape, _F32)
    for b in range(N_BUCKETS):
        out = jnp.where(bucket == b, rb_ref[b, h] * LOG2E, out)
    return out


def _reduce_rows(x, op):
    slab = 8
    parts = [x[r:r + slab] for r in range(0, x.shape[0], slab)]
    while len(parts) > 1:
        nxt = [op(parts[a], parts[a + 1]) for a in range(0, len(parts) - 1, 2)]
        if len(parts) % 2:
            nxt.append(parts[-1])
        parts = nxt
    red = jnp.max if op is jnp.maximum else jnp.sum
    return red(parts[0], axis=0, keepdims=True)


def _lambda(lq1_ref, lk1_ref, lq2_ref, lk2_ref, lambda_init):
    s1 = jnp.sum(lq1_ref[...] * lk1_ref[...], axis=-1, keepdims=True)
    s2 = jnp.sum(lq2_ref[...] * lk2_ref[...], axis=-1, keepdims=True)
    return jnp.exp(s1) - jnp.exp(s2) + lambda_init


def _bias_tiles_kernel(rb_ref, bucket_ref, bias_ref):
    h = pl.program_id(0)
    bq = bias_ref.shape[1]
    bias_ref[0] = _bias_from_buckets(bucket_ref[0], rb_ref, h)
    kc = lax.broadcasted_iota(jnp.int32, (bq, bq), 0) // CHUNK
    qc = lax.broadcasted_iota(jnp.int32, (bq, bq), 1) // CHUNK
    bias_ref[1] = jnp.where(kc <= qc, _bias_from_buckets(bucket_ref[1], rb_ref, h), -jnp.inf)


def _bias_tiles(rel_bias, bq):
    n_heads = rel_bias.shape[1]
    pos = jnp.arange(bq, dtype=jnp.int32)
    bucket = jnp.stack([_t5_bucket(pos[:, None] - (pos[None, :] + bq)),
                        _t5_bucket(pos[:, None] - pos[None, :])])
    return pl.pallas_call(
        _bias_tiles_kernel,
        grid=(n_heads,),
        in_specs=[pl.BlockSpec(memory_space=pltpu.SMEM),
                  _resident((2, bq, bq), lambda h: (0, 0, 0))],
        out_specs=pl.BlockSpec((None, 2, bq, bq), lambda h: (h, 0, 0, 0)),
        out_shape=jax.ShapeDtypeStruct((n_heads, 2, bq, bq), _F32),
        compiler_params=_params("parallel"),
        name="bias_tiles",
    )(rel_bias, bucket)


def _prompt_attn_kernel(rb_ref, lq1_ref, lk1_ref, lq2_ref, lk2_ref, q_ref, k_ref, vt_ref,
                        bias_ref, g_ref, o_ref,
                        qq_ref, s_ref, smax_ref, m_ref, l_ref, acc_ref, *, lambda_init):
    n_blocks, _, bq = vt_ref.shape
    h = pl.program_id(0)

    def block_rows(j):
        return pl.ds(pl.multiple_of(j * bq, bq), bq)

    tiles = list(range(0, 2 * bq, SCORE_COLS))

    def load_queries(i):
        qt = q_ref[i]
        zeros = jnp.zeros((HEAD_DIM, bq), _BF16)
        slot = i % 2
        qq_ref[slot, 0:HEAD_DIM, 0:bq] = qt[0:HEAD_DIM]
        qq_ref[slot, HEAD_DIM:, 0:bq] = zeros
        qq_ref[slot, 0:HEAD_DIM, bq:] = zeros
        qq_ref[slot, HEAD_DIM:, bq:] = qt[HEAD_DIM:]

    def issue_scores(i, j, bias_tile):
        k_blk = k_ref[block_rows(j), :]
        slot = i % 2

        def issue(n):
            c0 = tiles[n]
            s = _dot(k_blk, qq_ref[slot, :, c0:c0 + SCORE_COLS])
            if bias_tile is not None:
                q0 = c0 % bq
                s = s + bias_ref[bias_tile, :, q0:q0 + SCORE_COLS]
            s_ref[n] = s
            smax_ref[n] = _reduce_rows(s, jnp.maximum)

        return issue

    def block(j, shift, issue_next):
        vt = vt_ref[j]
        deferred = None
        for n, c0 in enumerate(tiles):
            cols = slice(c0, c0 + SCORE_COLS)
            m_old = m_ref[:, cols]
            m_new = jnp.maximum(m_old, smax_ref[n] + shift)
            p = jnp.exp2(s_ref[n] - (m_new - shift))
            alpha = jnp.exp2(m_old - m_new)
            l_ref[:, cols] = alpha * l_ref[:, cols] + _reduce_rows(p, jnp.add)
            m_ref[:, cols] = m_new
            pv = _dot(vt, p.astype(_BF16))
            if issue_next is not None:
                issue_next(n)
            if deferred is not None:
                dcols, dalpha, dpv = deferred
                acc_ref[:, dcols] = dalpha * acc_ref[:, dcols] + dpv
            deferred = (cols, alpha, pv)
        dcols, dalpha, dpv = deferred
        acc_ref[:, dcols] = dalpha * acc_ref[:, dcols] + dpv

    far_bias = rb_ref[FAR_BUCKET, h] * LOG2E
    lam = _lambda(lq1_ref, lk1_ref, lq2_ref, lk2_ref, lambda_init)

    def query_block(i, carry):
        nxt = jnp.minimum(i + 1, n_blocks - 1)
        load_queries(nxt)

        m_ref[...] = jnp.full(m_ref.shape, -jnp.inf, _F32)
        l_ref[...] = jnp.zeros(l_ref.shape, _F32)
        acc_ref[...] = jnp.zeros(acc_ref.shape, _F32)

        def far_then_far(j):
            block(j, far_bias, issue_scores(i, j + 1, None))

        first = 0
        left = jnp.maximum(i - 2, 0)
        for size in FAR_GROUPS:
            trips = left // size

            def far_group(t, c, size=size, first=first):
                for u in range(size):
                    far_then_far(first + size * t + u)
                return c

            lax.fori_loop(0, trips, far_group, 0)
            first = first + trips * size
            left = left - trips * size

        def finish():
            o = acc_ref[...] / l_ref[...]
            d = o[:, 0:bq] - lam * o[:, bq:]
            y = d * lax.rsqrt(jnp.mean(d * d, axis=0, keepdims=True) + SUBLN_EPS)
            o_ref[block_rows(i), :] = (y.T * g_ref[...] * (1.0 - lambda_init)).astype(_BF16)

        @pl.when(i >= 2)
        def _():
            block(i - 2, far_bias, issue_scores(i, i - 1, 0))
            block(i - 1, 0.0, issue_scores(i, i, 1))
            block(i, 0.0, issue_scores(i + 1, 0, None))
            finish()

        @pl.when(i == 1)
        def _():
            block(0, 0.0, issue_scores(1, 1, 1))
            block(1, 0.0, issue_scores(2, 0, None))
            finish()

        @pl.when(i == 0)
        def _():
            block(0, 0.0, issue_scores(1, 0, 0))
            finish()

        return carry

    qq_ref[1] = jnp.zeros(qq_ref.shape[1:], _BF16)
    load_queries(0)
    first_scores = issue_scores(0, 0, 1)
    for n in range(len(tiles)):
        first_scores(n)
    lax.fori_loop(0, n_blocks, query_block, 0)


def _attn_block(tp):
    bq = _pick_tile(tp, (512, 256))
    assert bq + 1 >= MAX_DISTANCE and bq % CHUNK == 0 and bq % SCORE_COLS == 0
    return bq


def _prompt_attention(q_t, kb, v_t, bias, rel_bias, lam_vecs, subln_g, layer, lambda_init):
    nq, d_attn, bq = q_t.shape
    tp = nq * bq
    n_heads = d_attn // HEAD_COLS
    lam_spec = _resident((None, 1, HEAD_DIM), lambda h: (layer, 0, 0))
    head_cols = pl.BlockSpec((tp, HEAD_COLS), lambda h: (0, h))
    head_cols_t = pl.BlockSpec((nq, HEAD_COLS, bq), lambda h: (0, h, 0))
    return pl.pallas_call(
        functools.partial(_prompt_attn_kernel, lambda_init=lambda_init),
        grid=(n_heads,),
        in_specs=[
            pl.BlockSpec(memory_space=pltpu.SMEM),
            lam_spec, lam_spec, lam_spec, lam_spec,
            head_cols_t, head_cols, head_cols_t,
            pl.BlockSpec((None, 2, bq, bq), lambda h: (h, 0, 0, 0)),
            _resident((None, 1, HEAD_COLS), lambda h: (layer, 0, 0)),
        ],
        out_specs=head_cols,
        out_shape=jax.ShapeDtypeStruct((tp, d_attn), _BF16),
        scratch_shapes=[
            pltpu.VMEM((2, HEAD_COLS, 2 * bq), _BF16),
            pltpu.VMEM((2 * bq // SCORE_COLS, bq, SCORE_COLS), _F32),
            pltpu.VMEM((2 * bq // SCORE_COLS, 1, SCORE_COLS), _F32),
            pltpu.VMEM((1, 2 * bq), _F32),
            pltpu.VMEM((1, 2 * bq), _F32),
            pltpu.VMEM((HEAD_COLS, 2 * bq), _F32),
        ],
        compiler_params=_params("parallel"),
        name="prompt_attention",
    )(rel_bias, *lam_vecs, q_t, kb, v_t, bias, subln_g)


def _sample_attn_kernel(rb_ref, lq1_ref, lk1_ref, lq2_ref, lk2_ref, q_ref, kn_ref, vn_ref,
                        kc_ref, vc_ref, bucket_c_ref, bucket_n_ref, g_ref, o_ref,
                        bias_c_ref, bias_n_ref, qq_ref, m_ref, l_ref, acc_ref, *, lambda_init):
    ts = q_ref.shape[0]
    n_heads = qq_ref.shape[0]
    chunk = kc_ref.shape[0] // n_heads
    b = pl.program_id(0)
    c = pl.program_id(1)

    @pl.when((b == 0) & (c == 0))
    def _():
        for h in range(n_heads):
            bias_n_ref[h] = _bias_from_buckets(bucket_n_ref[...], rb_ref, h)
            for cc in range(bias_c_ref.shape[1]):
                bias_c_ref[h, cc] = _bias_from_buckets(bucket_c_ref[cc], rb_ref, h)

    def update(h, s, v):
        m_old = m_ref[h]
        m_new = jnp.maximum(m_old, jnp.max(s, axis=-1, keepdims=True))
        p = jnp.exp2(s - m_new)
        alpha = jnp.exp2(m_old - m_new)
        l_ref[h] = alpha * l_ref[h] + jnp.sum(p, axis=-1, keepdims=True)
        acc_ref[h] = alpha * acc_ref[h] + _dot(p.astype(_BF16), v)
        m_ref[h] = m_new

    @pl.when(c == 0)
    def _():
        m_ref[...] = jnp.full(m_ref.shape, -jnp.inf, _F32)
        l_ref[...] = jnp.zeros(l_ref.shape, _F32)
        acc_ref[...] = jnp.zeros(acc_ref.shape, _F32)
        for h in range(n_heads):
            hc = slice(h * HEAD_COLS, (h + 1) * HEAD_COLS)
            q = q_ref[:, hc]
            lane = lax.broadcasted_iota(jnp.int32, q.shape, 1)
            zero = jnp.zeros_like(q)
            qq_ref[h] = jnp.concatenate([jnp.where(lane < HEAD_DIM, q, zero),
                                         jnp.where(lane >= HEAD_DIM, q, zero)], axis=0)
            bn = bias_n_ref[h]
            head_rows = pl.ds(h, ts, stride=n_heads)
            s = (_dot_nt(qq_ref[h], kn_ref[head_rows, :].astype(_BF16))
                 + jnp.concatenate([bn, bn], axis=0))
            update(h, s, vn_ref[head_rows, :].astype(_BF16))

    for h in range(n_heads):
        head_rows = pl.ds(h, chunk, stride=n_heads)
        bc = bias_c_ref[h, c]
        s = (_dot_nt(qq_ref[h], kc_ref[head_rows, :].astype(_BF16))
             + jnp.concatenate([bc, bc], axis=0))
        update(h, s, vc_ref[head_rows, :].astype(_BF16))

    @pl.when(c == pl.num_programs(1) - 1)
    def _():
        lam = _lambda(lq1_ref, lk1_ref, lq2_ref, lk2_ref, lambda_init)
        for h in range(n_heads):
            o = acc_ref[h] / l_ref[h]
            d = o[:ts] - lam * o[ts:]
            o_ref[:, h * HEAD_COLS:(h + 1) * HEAD_COLS] = (
                _rmsnorm(d, g_ref[...], SUBLN_EPS) * (1.0 - lambda_init)).astype(_BF16)


def _sample_attention(q, k, v, cache_k, cache_v, bs, ts, rel_bias, lam_vecs, subln_g,
                      layer, lambda_init):
    d_attn = q.shape[1]
    n_heads = d_attn // HEAD_COLS
    past = cache_k.shape[2] // n_heads
    assert ts % 16 == 0
    assert (past + ts - 1) // CHUNK <= past // CHUNK, "new frames must share the queries' chunk"
    chunk = _pick_tile(past, (1024, 512, 256, 128))
    n_chunks = past // chunk
    q_pos = past + jnp.arange(ts, dtype=jnp.int32)
    bucket_c = _t5_bucket(jnp.arange(past, dtype=jnp.int32)[None, :] - q_pos[:, None])
    bucket_c = bucket_c.reshape(ts, n_chunks, chunk).transpose(1, 0, 2)
    bucket_n = _t5_bucket(q_pos[None, :] - q_pos[:, None])
    new_q = pl.BlockSpec((ts, d_attn), lambda b, c: (b, 0))
    new_kv = pl.BlockSpec((ts * n_heads, HEAD_COLS), lambda b, c: (b, 0))
    cache = pl.BlockSpec((None, None, chunk * n_heads, HEAD_COLS), lambda b, c: (layer, b, c, 0))
    lam_spec = _resident((None, 1, HEAD_DIM), lambda b, c: (layer, 0, 0))
    return pl.pallas_call(
        functools.partial(_sample_attn_kernel, lambda_init=lambda_init),
        grid=(bs, n_chunks),
        in_specs=[
            pl.BlockSpec(memory_space=pltpu.SMEM),
            lam_spec, lam_spec, lam_spec, lam_spec,
            new_q, new_kv, new_kv, cache, cache,
            _resident((n_chunks, ts, chunk), lambda b, c: (0, 0, 0)),
            _resident((ts, ts), lambda b, c: (0, 0)),
            _resident((None, 1, HEAD_COLS), lambda b, c: (layer, 0, 0)),
        ],
        out_specs=pl.BlockSpec((ts, d_attn), lambda b, c: (b, 0)),
        out_shape=jax.ShapeDtypeStruct((bs * ts, d_attn), _BF16),
        scratch_shapes=[
            pltpu.VMEM((n_heads, n_chunks, ts, chunk), _F32),
            pltpu.VMEM((n_heads, ts, ts), _F32),
            pltpu.VMEM((n_heads, 2 * ts, HEAD_COLS), _BF16),
            pltpu.VMEM((n_heads, 2 * ts, 1), _F32),
            pltpu.VMEM((n_heads, 2 * ts, 1), _F32),
            pltpu.VMEM((n_heads, 2 * ts, HEAD_COLS), _F32),
        ],
        compiler_params=_params("arbitrary", "arbitrary"),
        name="sample_attention",
    )(rel_bias, *lam_vecs, q, k, v, cache_k, cache_v, bucket_c, bucket_n, subln_g)


def _merge_kernel(x_ref, c_ref, on_ref, g_ref, wga_ref, wgb_ref, wco_ref, wao_ref, wout_ref, y_ref):
    x = x_ref[...]
    h = _rmsnorm(x, g_ref[...], RMS_EPS).astype(_BF16)
    gate_a = jax.nn.sigmoid(_dot(h, wga_ref[...]))
    gate_b = jax.nn.sigmoid(_dot(h, wgb_ref[...]))
    merged = gate_a * _dot(c_ref[...], wco_ref[...]) + gate_b * _dot(on_ref[...], wao_ref[...])
    y_ref[...] = x + _dot(merged.astype(_BF16), wout_ref[...])


def _merge(x, c, o_n, norm_g, w_in_b, w_conv_out_b, w_attn_o_b, w_out_b, layer):
    r, d = x.shape
    tm = _pick_tile(r, (640, 512, 320, 256, 128, 64, 32, 16))
    gate_col0 = w_in_b.shape[2] // d - 2
    row = pl.BlockSpec((tm, d), lambda i: (i, 0))
    sq = _resident((None, d, d), lambda i: (layer, 0, 0))
    return pl.pallas_call(
        _merge_kernel,
        grid=(r // tm,),
        in_specs=[
            row, row, row,
            _resident((None, 1, d), lambda i: (layer, 0, 0)),
            _resident((None, d, d), lambda i: (layer, 0, gate_col0)),
            _resident((None, d, d), lambda i: (layer, 0, gate_col0 + 1)),
            sq, sq, sq,
        ],
        out_specs=row,
        out_shape=jax.ShapeDtypeStruct((r, d), _F32),
        compiler_params=_params("parallel"),
        name="merge_out_proj",
    )(x, c, o_n, norm_g, w_in_b, w_in_b, w_conv_out_b, w_attn_o_b, w_out_b)


def _ffn_kernel(x_ref, g_ref, wfi_ref, wfo_ref, gf_ref, y_ref, *, final_norm):
    d_ff = wfo_ref.shape[0]
    x = x_ref[...]
    h = _rmsnorm(x, g_ref[...], RMS_EPS).astype(_BF16)
    y = x
    half = (d_ff // 2) // 256 * 256
    for c0, c1 in ((0, half), (half, d_ff)):
        gate = _dot(h, wfi_ref[:, c0:c1])
        up = _dot(h, wfi_ref[:, d_ff + c0:d_ff + c1])
        act = (gate * jax.nn.sigmoid(gate) * up).astype(_BF16)
        y = y + _dot(act, wfo_ref[c0:c1, :])
    if final_norm:
        y = _rmsnorm(y, gf_ref[...], RMS_EPS)
    y_ref[...] = y


def _ffn(x, norm_g, w_ffn_in_b, w_ffn_out_b, norm_final, layer, final_norm):
    r, d = x.shape
    d_ff = w_ffn_out_b.shape[1]
    tm = _pick_tile(r, (640, 512, 320, 256, 128, 64, 32, 16))
    row = pl.BlockSpec((tm, d), lambda i: (i, 0))
    return pl.pallas_call(
        functools.partial(_ffn_kernel, final_norm=final_norm),
        grid=(r // tm,),
        in_specs=[
            row,
            _resident((None, 1, d), lambda i: (layer, 0, 0)),
            _resident((None, d, 2 * d_ff), lambda i: (layer, 0, 0)),
            _resident((None, d_ff, d), lambda i: (layer, 0, 0)),
            _resident((1, d), lambda i: (0, 0)),
        ],
        out_specs=row,
        out_shape=jax.ShapeDtypeStruct((r, d), _F32),
        compiler_params=_params("parallel"),
        name="ffn",
    )(x, norm_g, w_ffn_in_b, w_ffn_out_b, norm_final)


def kernel(x_prompt, x_sample, cache_k, cache_v, state_conv, norm_mix, w_in, conv_dw, conv_dw_b,
           conv_ln_g, conv_ln_b, w_conv_out, lambda_q1, lambda_k1, lambda_q2, lambda_k2,
           attn_subln_g, w_attn_o, w_out, norm_ffn, w_ffn_in, w_ffn_out, rel_bias, norm_final):
    depth = w_in.shape[0]
    bp, tp, d = x_prompt.shape
    bs, ts, _ = x_sample.shape
    past = cache_k.shape[2]
    n_heads = cache_k.shape[3]
    assert bp == 1, "one new stream per step"
    assert n_heads == N_HEADS and cache_k.shape[4] == HEAD_COLS
    assert rel_bias.shape == (N_BUCKETS, N_HEADS)
    assert ts >= CONV_STATE and ts % HALO_ROWS == 0 and tp % ts == 0
    dc = conv_dw.shape[2]

    xp = x_prompt.reshape(tp, d)
    xs = x_sample.reshape(bs * ts, d)

    as_rows = lambda a: a.reshape(depth, 1, a.shape[-1])
    norm_mix3, norm_ffn3 = as_rows(norm_mix), as_rows(norm_ffn)
    conv_b3, ln_g3, ln_b3 = as_rows(conv_dw_b), as_rows(conv_ln_g), as_rows(conv_ln_b)
    subln3 = as_rows(attn_subln_g)
    lam3 = [as_rows(a) for a in (lambda_q1, lambda_k1, lambda_q2, lambda_k2)]
    norm_final2 = norm_final.reshape(1, d)
    rel_bias = rel_bias.astype(_F32)
    w_in_b = w_in.astype(_BF16)
    w_conv_out_b = w_conv_out.astype(_BF16)
    w_attn_o_b = w_attn_o.astype(_BF16)
    w_out_b = w_out.astype(_BF16)
    w_ffn_in_b = w_ffn_in.astype(_BF16)
    w_ffn_out_b = w_ffn_out.astype(_BF16)
    cache_k4 = cache_k.reshape(depth, bs, past * n_heads, HEAD_COLS)
    cache_v4 = cache_v.reshape(depth, bs, past * n_heads, HEAD_COLS)
    halo_s = jnp.pad(state_conv, ((0, 0), (0, 0), (HALO_ROWS - CONV_STATE, 0), (0, 0)))
    halo_s = halo_s.reshape(depth, bs * HALO_ROWS, dc)

    tm_conv = _pick_tile(tp, (128, 64, 32))
    halo_per_tile = tm_conv // HALO_ROWS
    bq = _attn_block(tp)
    attn_bias = _bias_tiles(rel_bias, bq)

    k_all = jnp.zeros((depth * tp * n_heads, HEAD_COLS), _F32)
    v_all = jnp.zeros((depth * tp * n_heads, HEAD_COLS), _F32)
    c_p, k_s, v_s, c_s = [], [], [], []
    for layer in range(depth):
        lambda_init = 0.8 - 0.6 * math.exp(-0.3 * layer)
        last = layer == depth - 1
        conv_args = (conv_dw, conv_b3, ln_g3, ln_b3, layer)
        merge_args = (norm_mix3, w_in_b, w_conv_out_b, w_attn_o_b, w_out_b, layer)
        ffn_args = (norm_ffn3, w_ffn_in_b, w_ffn_out_b, norm_final2, layer, last)

        glu, q_t, k_all, v_all, kb, v_t = _in_proj(xp, norm_mix3, w_in_b, layer, (k_all, v_all),
                                                  block_t=bq)
        c = _conv_branch(glu, glu, tm_conv, tp // tm_conv,
                         lambda i: (jnp.maximum(i * halo_per_tile - 1, 0), 0), True, *conv_args)
        o_n = _prompt_attention(q_t, kb, v_t, attn_bias, rel_bias, lam3, subln3, layer,
                                lambda_init)
        xp = _ffn(_merge(xp, c, o_n, *merge_args), *ffn_args)
        c_p.append(glu[tp - CONV_STATE:].reshape(1, CONV_STATE, dc))

        glu, q, k, v, _, _ = _in_proj(xs, norm_mix3, w_in_b, layer)
        c = _conv_branch(glu, halo_s[layer], ts, bs, lambda i: (i, 0), False, *conv_args)
        o_n = _sample_attention(q, k, v, cache_k4, cache_v4, bs, ts, rel_bias, lam3, subln3,
                                layer, lambda_init)
        xs = _ffn(_merge(xs, c, o_n, *merge_args), *ffn_args)
        k_s.append(k.reshape(bs, ts, n_heads, HEAD_COLS))
        v_s.append(v.reshape(bs, ts, n_heads, HEAD_COLS))
        c_s.append(glu.reshape(bs, ts, dc)[:, ts - CONV_STATE:])

    kv_shape = (depth, 1, tp, n_heads, HEAD_COLS)
    return (xp.reshape(1, tp, d), xs.reshape(bs, ts, d), k_all.reshape(kv_shape),
            v_all.reshape(kv_shape), jnp.stack(c_p), jnp.stack(k_s), jnp.stack(v_s),
            jnp.stack(c_s))
```
